```python
import math
import jax, jax.numpy as jnp
from jax import lax
import numpy as np

D_MODEL = 1024
BATCH = 16
SEQ = 2048
DEPTH = 2

MEM_LEN = 256
BRANCH_WIDTH = D_MODEL // 2
N_BRANCH = 3
DA_QK_DIM = 64
DA_V_DIM = 2 * DA_QK_DIM
DA_WIDTH = BRANCH_WIDTH
DA_HEADS = DA_WIDTH // DA_V_DIM
DA_QK_WIDTH = 2 * DA_HEADS * DA_QK_DIM
RW_HEAD = 64
RW_WIDTH = BRANCH_WIDTH
RW_HEADS = RW_WIDTH // RW_HEAD
RW_DECAY_LORA = 64
RW_AAA_LORA = 64
RW_SHIFT_WIDTH = 3 * RW_WIDTH + RW_DECAY_LORA + RW_AAA_LORA
MU_SPLITS = (RW_WIDTH, 2 * RW_WIDTH, 3 * RW_WIDTH, 3 * RW_WIDTH + RW_DECAY_LORA)
CA_HEADS = 4
CA_WIDTH = BRANCH_WIDTH
CA_HEAD_DIM = CA_WIDTH // CA_HEADS
ROPE_THETA = 500000.0
ROPE_FRAC = 4
Q_BLOCK = 128
NORM_EPS = 1e-6
GN_EPS = 64e-5

IN_SIZES = (
    DA_QK_WIDTH, DA_QK_WIDTH, DA_WIDTH, DA_WIDTH,
    RW_WIDTH, RW_WIDTH, RW_WIDTH, RW_DECAY_LORA, RW_AAA_LORA,
    RW_WIDTH,
    CA_WIDTH, CA_WIDTH,
    N_BRANCH * D_MODEL,
)
IN_WIDTH = sum(IN_SIZES)
IN_SPLITS = tuple(sum(IN_SIZES[:i + 1]) for i in range(len(IN_SIZES) - 1))

kernel_name = 'hybrid_diffattn_rwkv7_memory_block'


def _rms(x, g, eps=NORM_EPS):
    xf = x.astype(jnp.float32)
    y = xf * lax.rsqrt(jnp.mean(jnp.square(xf), axis=-1, keepdims=True) + eps)
    return (y * g.astype(jnp.float32)).astype(x.dtype)


def _rope_tables(seq, dim):
    rot = dim // ROPE_FRAC
    inv = 1.0 / (ROPE_THETA ** (jnp.arange(0, rot, 2, dtype=jnp.float32) / rot))
    ang = jnp.arange(seq, dtype=jnp.float32)[:, None] * inv[None, :]
    return jnp.cos(ang), jnp.sin(ang)


def _partial_rope(t, cos, sin):
    half = cos.shape[-1]
    tf = t.astype(jnp.float32)
    t1, t2, rest = tf[..., :half], tf[..., half:2 * half], tf[..., 2 * half:]
    c = cos[:, None, None, :]
    s = sin[:, None, None, :]
    out = jnp.concatenate([t1 * c - t2 * s, t2 * c + t1 * s, rest], axis=-1)
    return out.astype(t.dtype)


def _token_shift(p, mu):
    prev = jnp.pad(p, ((0, 0), (1, 0), (0, 0)))[:, :-1]
    return p + (prev - p) * mu


def _diff_attention_branch(q, k, v, z, q_g, k_g, lam_vecs, subln_g, lam_init, cos, sin):
    B, S, _ = q.shape
    q = _partial_rope(_rms(q.reshape(B, S, DA_HEADS, 2, DA_QK_DIM), q_g), cos, sin)
    k = _partial_rope(_rms(k.reshape(B, S, DA_HEADS, 2, DA_QK_DIM), k_g), cos, sin)
    v = v.reshape(B, S, DA_HEADS, DA_V_DIM)
    lv = lam_vecs.astype(jnp.float32)
    lam = jnp.exp(jnp.sum(lv[0] * lv[1])) - jnp.exp(jnp.sum(lv[2] * lv[3])) + lam_init
    scale = DA_QK_DIM ** -0.5
    outs = []
    for i in range(S // Q_BLOCK):
        s0, s1 = i * Q_BLOCK, (i + 1) * Q_BLOCK
        qb, kb, vb = q[:, s0:s1], k[:, :s1], v[:, :s1]
        sc = jnp.einsum('bqhcd,bkhcd->bhcqk', qb, kb).astype(jnp.float32) * scale
        mask = (s0 + jnp.arange(Q_BLOCK))[:, None] >= jnp.arange(s1)[None, :]
        sc = jnp.where(mask, sc, -jnp.inf)
        p = jax.nn.softmax(sc, axis=-1)
        wdiff = p[:, :, 0] - lam * p[:, :, 1]
        outs.append(jnp.einsum('bhqk,bkhd->bqhd', wdiff.astype(vb.dtype), vb))
    o = jnp.concatenate(outs, axis=1)
    o = _rms(o, subln_g) * (1.0 - lam_init)
    return o.reshape(B, S, DA_WIDTH) * jax.nn.silu(z)


def _rwkv7_scan(r, w, k, v, a_vec, b_vec):
    B, S, H, N = r.shape
    xs = tuple(jnp.moveaxis(t, 1, 0) for t in (r, w, k, v, a_vec, b_vec))

    def step(state, inp):
        r_t, w_t, k_t, v_t, a_t, b_t = inp
        sa = jnp.einsum('bhvk,bhk->bhv', state, a_t)
        state = (state * w_t[:, :, None, :] + sa[..., None] * b_t[:, :, None, :]
                 + v_t[..., None] * k_t[:, :, None, :])
        return state, jnp.einsum('bhvk,bhk->bhv', state, r_t)

    s0 = jnp.zeros((B, H, N, N), jnp.float32)
    _, ys = lax.scan(step, s0, xs)
    return jnp.moveaxis(ys, 0, 1)


def _rwkv7_branch(r, k, v, wl, al, z, mu, w0, w_up, a0, a_up, k_k, k_a, r_k, ln_g, ln_b):
    B, S, _ = r.shape
    f32 = jnp.float32
    mu_r, mu_k, mu_v, mu_w, mu_a = jnp.split(mu, MU_SPLITS)
    r = _token_shift(r, mu_r)
    k = _token_shift(k, mu_k)
    v = _token_shift(v, mu_v)
    wl = _token_shift(wl, mu_w)
    al = _token_shift(al, mu_a)
    heads = lambda t: t.reshape(B, S, RW_HEADS, RW_HEAD)
    w = -jax.nn.softplus(-(w0 + jnp.tanh(wl) @ w_up).astype(f32)) - 0.5
    decay = jnp.exp(-jnp.exp(w))
    a = jax.nn.sigmoid((a0 + al @ a_up).astype(f32))
    rf, kf, vf = r.astype(f32), k.astype(f32), v.astype(f32)
    kk = heads(kf * k_k.astype(f32))
    kk = kk / jnp.maximum(jnp.sqrt(jnp.sum(kk * kk, axis=-1, keepdims=True)), 1e-12)
    kf = kf * (1.0 + (a - 1.0) * k_a.astype(f32))
    y = _rwkv7_scan(heads(rf), heads(decay), heads(kf), heads(vf), -kk, kk * heads(a))
    mean = jnp.mean(y, axis=-1, keepdims=True)
    var = jnp.mean(jnp.square(y - mean), axis=-1, keepdims=True)
    y = ((y - mean) * lax.rsqrt(var + GN_EPS)).reshape(B, S, RW_WIDTH)
    y = y * ln_g.astype(f32) + ln_b.astype(f32)
    bonus = jnp.sum(heads(rf) * heads(kf) * r_k.astype(f32), axis=-1, keepdims=True) * heads(vf)
    y = y + bonus.reshape(B, S, RW_WIDTH)
    return y.astype(z.dtype) * jax.nn.silu(z)


def _memory_cross_attention_branch(q, z, mem_kv, q_g, k_g):
    B, S, _ = q.shape
    M = mem_kv.shape[1]
    q = _rms(q.reshape(B, S, CA_HEADS, CA_HEAD_DIM), q_g)
    kv = mem_kv.reshape(B, M, 2, CA_HEADS, CA_HEAD_DIM)
    km = _rms(kv[:, :, 0], k_g)
    vm = kv[:, :, 1]
    sc = jnp.einsum('bshd,bmhd->bhsm', q, km).astype(jnp.float32) * (CA_HEAD_DIM ** -0.5)
    p = jax.nn.softmax(sc, axis=-1).astype(vm.dtype)
    o = jnp.einsum('bhsm,bmhd->bshd', p, vm).reshape(B, S, CA_WIDTH)
    return o * jax.nn.silu(z)


def setup_inputs(seed: int = 0) -> dict:
    key = jax.random.key(seed)
    ks = jax.random.split(key, 24)
    L = DEPTH
    n = lambda k, shape, s: s * jax.random.normal(k, shape, jnp.float32)
    return {
        'x': n(ks[0], (BATCH, SEQ, D_MODEL), 1.0),
        'mem': n(ks[1], (BATCH, MEM_LEN, D_MODEL), 1.0),
        'norm_g': 1.0 + n(ks[2], (L, D_MODEL), 0.05),
        'mem_norm_g': 1.0 + n(ks[3], (L, D_MODEL), 0.05),
        'w_in': n(ks[4], (L, D_MODEL, IN_WIDTH), D_MODEL ** -0.5),
        'w_mem_kv': n(ks[5], (L, D_MODEL, 2 * CA_WIDTH), D_MODEL ** -0.5),
        'da_q_norm': 1.0 + n(ks[6], (L, DA_QK_DIM), 0.05),
        'da_k_norm': 1.0 + n(ks[7], (L, DA_QK_DIM), 0.05),
        'da_lambda': n(ks[8], (L, 4, DA_QK_DIM), 0.1),
        'da_subln': 1.0 + n(ks[9], (L, DA_V_DIM), 0.05),
        'rw_mu': jax.random.uniform(ks[10], (L, RW_SHIFT_WIDTH), jnp.float32),
        'rw_w0': -3.0 + n(ks[11], (L, RW_WIDTH), 1.0),
        'rw_w_up': n(ks[12], (L, RW_DECAY_LORA, RW_WIDTH), 0.1),
        'rw_a0': n(ks[13], (L, RW_WIDTH), 0.1),
        'rw_a_up': n(ks[14], (L, RW_AAA_LORA, RW_WIDTH), RW_AAA_LORA ** -0.5),
        'rw_k_k': 0.85 + n(ks[15], (L, RW_WIDTH), 0.05),
        'rw_k_a': 1.0 + n(ks[16], (L, RW_WIDTH), 0.05),
        'rw_r_k': n(ks[17], (L, RW_HEADS, RW_HEAD), 0.1),
        'rw_ln_g': 1.0 + n(ks[18], (L, RW_WIDTH), 0.05),
        'rw_ln_b': n(ks[19], (L, RW_WIDTH), 0.02),
        'ca_q_norm': 1.0 + n(ks[20], (L, CA_HEAD_DIM), 0.05),
        'ca_k_norm': 1.0 + n(ks[21], (L, CA_HEAD_DIM), 0.05),
        'w_branch': n(ks[22], (L, N_BRANCH, BRANCH_WIDTH, D_MODEL), BRANCH_WIDTH ** -0.5),
        'w_out': n(ks[23], (L, D_MODEL, D_MODEL), D_MODEL ** -0.5),
    }


def reference(x, mem, norm_g, mem_norm_g, w_in, w_mem_kv, da_q_norm, da_k_norm, da_lambda,
              da_subln, rw_mu, rw_w0, rw_w_up, rw_a0, rw_a_up, rw_k_k, rw_k_a, rw_r_k,
              rw_ln_g, rw_ln_b, ca_q_norm, ca_k_norm, w_branch, w_out):
    B, S, _ = x.shape
    cos, sin = _rope_tables(S, DA_QK_DIM)
    for l in range(DEPTH):
        h = _rms(x, norm_g[l])
        proj = h @ w_in[l]
        (da_q, da_k, da_v, da_z, rw_r, rw_k, rw_v, rw_wl, rw_al, rw_z,
         ca_q, ca_z, gate_logits) = jnp.split(proj, IN_SPLITS, axis=-1)
        lam_init = 0.8 - 0.6 * math.exp(-0.3 * l)
        y_a = _diff_attention_branch(da_q, da_k, da_v, da_z, da_q_norm[l], da_k_norm[l],
                                     da_lambda[l], da_subln[l], lam_init, cos, sin)
        y_b = _rwkv7_branch(rw_r, rw_k, rw_v, rw_wl, rw_al, rw_z, rw_mu[l], rw_w0[l],
                            rw_w_up[l], rw_a0[l], rw_a_up[l], rw_k_k[l], rw_k_a[l],
                            rw_r_k[l], rw_ln_g[l], rw_ln_b[l])
        mem_kv = _rms(mem, mem_norm_g[l]) @ w_mem_kv[l]
        y_c = _memory_cross_attention_branch(ca_q, ca_z, mem_kv, ca_q_norm[l], ca_k_norm[l])
        gates = jax.nn.sigmoid(gate_logits.reshape(B, S, N_BRANCH, D_MODEL))
        branches = jnp.stack([y_a, y_b, y_c], axis=2)
        branch_proj = jnp.einsum('bsnc,ncd->bsnd', branches, w_branch[l])
        merged = jnp.sum(gates * branch_proj, axis=2)
        x = x + merged @ w_out[l]
    return x
```

```python
import functools
import math

import jax
import jax.numpy as jnp
from jax import lax
from jax.experimental import pallas as pl
from jax.experimental.pallas import tpu as pltpu

F32 = jnp.float32
BF16 = jnp.bfloat16

D_MODEL = 1024
WIDTH = 512
DA_HEADS = 4
DA_QK_DIM = 64
DA_V_DIM = 128
RW_HEADS = 8
RW_HEAD = 64
RW_LORA = 64
CA_HEADS = 4
CA_HEAD_DIM = 128
ROPE_THETA = 500000.0
ROPE_HALF = DA_QK_DIM // 8
NORM_EPS = 1e-6
GN_EPS = 64e-5

LANES = 128
MAIN_WIDTH = 8192
VMEM_LIMIT = 56 * 1024 * 1024

COL_DA_Q, COL_DA_K, COL_DA_V, COL_DA_Z = 0, 1, 2, 3
COL_RW_R, COL_RW_K, COL_RW_V, COL_RW_Z = 4, 5, 6, 7
COL_CA_Q, COL_CA_Z = 8, 9
COL_GATES = 10

ROW_TILE = 512
ATT_TILE = 256
CHUNK = 64
MEM_Q_TILE = 512


def _params(*sem):
    return pltpu.CompilerParams(dimension_semantics=sem, vmem_limit_bytes=VMEM_LIMIT)


def _resident(shape):
    nd = len(shape)
    return pl.BlockSpec(shape, lambda *_: (0,) * nd, pipeline_mode=pl.Buffered(1))


def _silu(z):
    return z * (1.0 / (1.0 + jnp.exp(-z)))


def _sigmoid(z):
    return 1.0 / (1.0 + jnp.exp(-z))


def _split3(x):
    hi = x.astype(BF16)
    r1 = x - hi.astype(F32)
    mid = r1.astype(BF16)
    lo = (r1 - mid.astype(F32)).astype(BF16)
    return hi, mid, lo


def _dot(a, b):
    return jnp.dot(a, b, preferred_element_type=F32)


def _dot_nt(a, b):
    return lax.dot_general(a, b, (((1,), (1,)), ((), ())), preferred_element_type=F32)


def _dot_tn(a, b):
    return lax.dot_general(a, b, (((0,), (0,)), ((), ())), preferred_element_type=F32)


def _in_proj_kernel(x_ref, g_ref, w_ref, wlo_ref, qg_ref, kg_ref, cos_ref, sinm_ref, sinp_ref,
                    bd_ref, main_ref, lora_ref):
    x = x_ref[...]
    inv = lax.rsqrt(jnp.mean(x * x, axis=-1, keepdims=True) + NORM_EPS)
    h = (x * inv * g_ref[...]).astype(BF16)
    cosf, sinm, sinp = cos_ref[...], sinm_ref[...], sinp_ref[...]
    for c in range(MAIN_WIDTH // WIDTH):
        acc = _dot(h, w_ref[:, c * WIDTH:(c + 1) * WIDTH])
        if c in (COL_DA_Q, COL_DA_K):
            gain = qg_ref[...] if c == COL_DA_Q else kg_ref[...]
            ssq = _dot((acc * acc).astype(BF16), bd_ref[...])
            acc = acc * lax.rsqrt(ssq * (1.0 / DA_QK_DIM) + NORM_EPS) * gain
            for s in range(WIDTH // LANES):
                t = acc[:, s * LANES:(s + 1) * LANES]
                t = (t * cosf + pltpu.roll(t, LANES - ROPE_HALF, 1) * sinm
                     + pltpu.roll(t, ROPE_HALF, 1) * sinp)
                lo = c * WIDTH + s * LANES
                main_ref[:, lo:lo + LANES] = t.astype(BF16)
        else:
            main_ref[:, c * WIDTH:(c + 1) * WIDTH] = acc.astype(BF16)
    lora_ref[...] = _dot(h, wlo_ref[...])


def _in_proj(x2, g, w_main, w_lora, qg, kg, cosf, sinm, sinp, bd, seq):
    tokens = x2.shape[0]
    tm = ROW_TILE
    pos_blocks = seq // tm
    row = lambda i: (i, 0)
    pos = lambda i: (i % pos_blocks, 0)
    return pl.pallas_call(
        _in_proj_kernel,
        grid=(tokens // tm,),
        in_specs=[
            pl.BlockSpec((tm, D_MODEL), row),
            _resident((1, D_MODEL)),
            _resident((D_MODEL, MAIN_WIDTH)),
            _resident((D_MODEL, LANES)),
            _resident((1, WIDTH)),
            _resident((1, WIDTH)),
            pl.BlockSpec((tm, LANES), pos),
            pl.BlockSpec((tm, LANES), pos),
            pl.BlockSpec((tm, LANES), pos),
            _resident((WIDTH, WIDTH)),
        ],
        out_specs=[pl.BlockSpec((tm, MAIN_WIDTH), row), pl.BlockSpec((tm, LANES), row)],
        out_shape=[jax.ShapeDtypeStruct((tokens, MAIN_WIDTH), BF16),
                   jax.ShapeDtypeStruct((tokens, LANES), F32)],
        compiler_params=_params("parallel"),
        name="in_proj",
    )(x2, g, w_main, w_lora, qg, kg, cosf, sinm, sinp, bd)


def _diff_attn_kernel(lam_ref, q_ref, k_ref, v_ref, z_ref, sub_ref, o_ref, m_sc, l_sc, acc_sc,
                      *, lam_init):
    t = ATT_TILE
    i = pl.program_id(2)
    q = q_ref[...]
    lane = lax.broadcasted_iota(jnp.int32, (t, LANES), 1)
    zero = jnp.zeros_like(q)
    qs = jnp.concatenate([jnp.where(lane < DA_QK_DIM, q, zero),
                          jnp.where(lane >= DA_QK_DIM, q, zero)], axis=0)
    m_sc[...] = jnp.full(m_sc.shape, -jnp.inf, F32)
    l_sc[...] = jnp.zeros(l_sc.shape, F32)
    acc_sc[...] = jnp.zeros(acc_sc.shape, F32)

    def step(j, masked):
        start = pl.multiple_of(j * t, t)
        k = k_ref[pl.ds(start, t), :]
        v = v_ref[pl.ds(start, t), :]
        s = _dot_nt(qs, k)
        if masked:
            r = lax.broadcasted_iota(jnp.int32, (2 * t, t), 0)
            r = jnp.where(r >= t, r - t, r)
            c = lax.broadcasted_iota(jnp.int32, (2 * t, t), 1)
            s = jnp.where(r >= c, s, -jnp.inf)
        m_prev = m_sc[...]
        m_new = jnp.maximum(m_prev, jnp.max(s, axis=-1, keepdims=True))
        alpha = jnp.exp(m_prev - m_new)
        p = jnp.exp(s - m_new)
        l_sc[...] = alpha * l_sc[...] + jnp.sum(p, axis=-1, keepdims=True)
        acc_sc[...] = alpha * acc_sc[...] + _dot(p.astype(BF16), v)
        m_sc[...] = m_new

    def body(j, carry):
        step(j, False)
        return carry

    lax.fori_loop(0, i, body, 0)
    step(i, True)

    lv = lam_ref[...]
    lam = (jnp.exp(jnp.sum(lv[0:1] * lv[1:2], axis=-1, keepdims=True))
           - jnp.exp(jnp.sum(lv[2:3] * lv[3:4], axis=-1, keepdims=True)) + lam_init)
    acc = acc_sc[...]
    l = l_sc[...]
    o = acc[:t] / l[:t] - lam * (acc[t:] / l[t:])
    o = o * lax.rsqrt(jnp.mean(o * o, axis=-1, keepdims=True) + NORM_EPS) * sub_ref[...]
    o = o * (1.0 - lam_init)
    z = z_ref[...].astype(F32)
    o_ref[...] = (o * _silu(z)).astype(BF16)


def _diff_attn(main, lam_vecs, subln, batch, seq, lam_init):
    t = ATT_TILE
    nq = seq // t
    per_head = WIDTH // LANES
    qmap = lambda b, h, i: (b * nq + i, COL_DA_Q * per_head + h)
    kmap = lambda b, h, i: (b, COL_DA_K * per_head + h)
    vmap = lambda b, h, i: (b, COL_DA_V * per_head + h)
    zmap = lambda b, h, i: (b * nq + i, COL_DA_Z * per_head + h)
    omap = lambda b, h, i: (b * nq + i, h)
    return pl.pallas_call(
        functools.partial(_diff_attn_kernel, lam_init=lam_init),
        grid=(batch, DA_HEADS, nq),
        in_specs=[
            _resident((4, DA_QK_DIM)),
            pl.BlockSpec((t, LANES), qmap),
            pl.BlockSpec((seq, LANES), kmap),
            pl.BlockSpec((seq, LANES), vmap),
            pl.BlockSpec((t, LANES), zmap),
            _resident((1, DA_V_DIM)),
        ],
        out_specs=pl.BlockSpec((t, LANES), omap),
        out_shape=jax.ShapeDtypeStruct((batch * seq, WIDTH), BF16),
        scratch_shapes=[pltpu.VMEM((2 * t, 1), F32), pltpu.VMEM((2 * t, 1), F32),
                        pltpu.VMEM((2 * t, DA_V_DIM), F32)],
        compiler_params=_params("parallel", "parallel", "arbitrary"),
        name="diff_attn",
    )(lam_vecs, main, main, main, main, subln)


def _rwkv_kernel(r_ref, k_ref, v_ref, z_ref, lo_ref, mur_ref, muk_ref, muv_ref, mulo_ref,
                 w0_ref, a0_ref, kk_ref, ka_ref, rk_ref, lng_ref, lnb_ref,
                 wup_hi_ref, wup_lo_ref, aup_hi_ref, aup_lo_ref, bd_ref, tri_ref,
                 o_ref, cr_sc, ck_sc, cv_sc, clo_sc, state_sc):
    n = CHUNK

    @pl.when(pl.program_id(1) == 0)
    def _():
        cr_sc[...] = jnp.zeros(cr_sc.shape, F32)
        ck_sc[...] = jnp.zeros(ck_sc.shape, F32)
        cv_sc[...] = jnp.zeros(cv_sc.shape, F32)
        clo_sc[...] = jnp.zeros(clo_sc.shape, F32)
        state_sc[...] = jnp.zeros(state_sc.shape, F32)

    def shift(p, carry_ref, mu):
        first = lax.broadcasted_iota(jnp.int32, p.shape, 0) == 0
        prev = jnp.where(first, carry_ref[...], pltpu.roll(p, 1, 0))
        carry_ref[...] = p[n - 1:n, :]
        return p + (prev - p) * mu

    r = shift(r_ref[...].astype(F32), cr_sc, mur_ref[...])
    k = shift(k_ref[...].astype(F32), ck_sc, muk_ref[...])
    v = shift(v_ref[...].astype(F32), cv_sc, muv_ref[...])
    lo = shift(lo_ref[...], clo_sc, mulo_ref[...])

    def dot_hi(x, w_hi_ref, w_lo_ref):
        xh = x.astype(BF16)
        xl = (x - xh.astype(F32)).astype(BF16)
        w_hi = w_hi_ref[...]
        return _dot(xh, w_hi) + _dot(xl, w_hi) + _dot(xh, w_lo_ref[...])

    wlog = w0_ref[...] + dot_hi(jnp.tanh(lo), wup_hi_ref, wup_lo_ref)
    alog = a0_ref[...] + dot_hi(lo, aup_hi_ref, aup_lo_ref)
    tneg = -wlog
    softplus = jnp.maximum(tneg, 0.0) + jnp.log(1.0 + jnp.exp(-jnp.abs(tneg)))
    logdecay = -jnp.exp(-softplus - 0.5)
    a = _sigmoid(alog)

    bd = bd_ref[...]
    kk = k * kk_ref[...]
    kk2 = kk * kk
    kk2h = kk2.astype(BF16)
    kk2l = (kk2 - kk2h.astype(F32)).astype(BF16)
    ssq = _dot(kk2h, bd) + _dot(kk2l, bd)
    kk = kk / jnp.maximum(jnp.sqrt(ssq), 1e-12)
    kmod = k * (1.0 + (a - 1.0) * ka_ref[...])

    tri = tri_ref[...]
    d_hi, d_mid, d_lo = _split3(logdecay)
    cum = _dot(tri, d_hi) + _dot(tri, d_mid) + _dot(tri, d_lo)
    e_in = jnp.exp(cum)
    e_ex = jnp.exp(cum - logdecay)
    e_inv = jnp.exp(-cum)
    p_end = e_in[n - 1:n, :]
    a_t = -kk * e_ex
    r_t = r * e_in
    b_t = kk * a * e_inv
    k_t = kmod * e_inv
    rkb = r * kmod * rk_ref[...]
    z = z_ref[...].astype(F32)
    lng, lnb = lng_ref[...], lnb_ref[...]

    lane = lax.broadcasted_iota(jnp.int32, (n, LANES), 1)
    first_head = lane < RW_HEAD

    def stack(xp):
        zero = jnp.zeros_like(xp)
        return jnp.concatenate([jnp.where(first_head, xp, zero),
                                jnp.where(first_head, zero, xp)], axis=0)

    ri = lax.broadcasted_iota(jnp.int32, (2 * n, 2 * n), 0)
    ci = lax.broadcasted_iota(jnp.int32, (2 * n, 2 * n), 1)
    ri = jnp.where(ri >= n, ri - n, ri)
    ci = jnp.where(ci >= n, ci - n, ci)
    strict = ri > ci
    incl = ri >= ci
    eye = (lax.broadcasted_iota(jnp.int32, (2 * n, 2 * n), 0)
           == lax.broadcasted_iota(jnp.int32, (2 * n, 2 * n), 1)).astype(F32)
    srow = lax.broadcasted_iota(jnp.int32, (2 * n, LANES), 0)
    slane = lax.broadcasted_iota(jnp.int32, (2 * n, LANES), 1)
    own = (srow < n) == (slane < RW_HEAD)

    for p in range(RW_HEADS // 2):
        sl = slice(p * LANES, (p + 1) * LANES)
        a_s = stack(a_t[:, sl]).astype(BF16)
        r_s = stack(r_t[:, sl]).astype(BF16)
        b_s = stack(b_t[:, sl])
        k_s = stack(k_t[:, sl])
        v_s = stack(v[:, sl])
        pe = p_end[:, sl]
        bk_in = jnp.concatenate([b_s, k_s], axis=0).astype(BF16)
        m = _dot_nt(jnp.concatenate([a_s, r_s], axis=0), bk_in)
        a_ab = jnp.where(strict, m[:2 * n, :2 * n], 0.0)
        a_ak = jnp.where(strict, m[:2 * n, 2 * n:], 0.0)
        t_rb = jnp.where(incl, m[2 * n:, :2 * n], 0.0)
        t_rk = jnp.where(incl, m[2 * n:, 2 * n:], 0.0)
        x = a_ab
        tinv = eye + x
        for _ in range(int(math.log2(n)) - 1):
            xb = x.astype(BF16)
            x = _dot(xb, xb)
            tinv = tinv + _dot(tinv.astype(BF16), x.astype(BF16))
        gt = state_sc[p]
        gtb = gt.astype(BF16)
        v_sb = v_s.astype(BF16)
        u = _dot(tinv.astype(BF16),
                 (_dot_nt(a_s, gtb) + _dot(a_ak.astype(BF16), v_sb)).astype(BF16))
        ub = u.astype(BF16)
        y = _dot_nt(r_s, gtb) + _dot(t_rb.astype(BF16), ub) + _dot(t_rk.astype(BF16), v_sb)
        uv = jnp.concatenate([ub, v_sb], axis=0)
        bk_out = jnp.concatenate([b_s * pe, k_s * pe], axis=0).astype(BF16)
        state_sc[p] = gt * pe + _dot_tn(uv, bk_out)

        mean = jnp.sum(y, axis=-1, keepdims=True) * (1.0 / RW_HEAD)
        d = jnp.where(own, y - mean, 0.0)
        var = jnp.sum(d * d, axis=-1, keepdims=True) * (1.0 / RW_HEAD)
        yn = d * lax.rsqrt(var + GN_EPS)
        bonus = jnp.sum(stack(rkb[:, sl]), axis=-1, keepdims=True) * v_s
        yp = yn[:n] + yn[n:]
        bp = bonus[:n] + bonus[n:]
        out = (yp * lng[:, sl] + lnb[:, sl] + bp) * _silu(z[:, sl])
        o_ref[:, sl] = out.astype(BF16)


def _rwkv(main, lora, prm, batch, seq):
    n = CHUNK
    nc = seq // n
    rmap = lambda col: (lambda b, c: (b * nc + c, col))
    vec = lambda: _resident((1, WIDTH))
    return pl.pallas_call(
        _rwkv_kernel,
        grid=(batch, nc),
        in_specs=[
            pl.BlockSpec((n, WIDTH), rmap(COL_RW_R)),
            pl.BlockSpec((n, WIDTH), rmap(COL_RW_K)),
            pl.BlockSpec((n, WIDTH), rmap(COL_RW_V)),
            pl.BlockSpec((n, WIDTH), rmap(COL_RW_Z)),
            pl.BlockSpec((n, LANES), rmap(0)),
            vec(), vec(), vec(), _resident((1, LANES)),
            vec(), vec(), vec(), vec(), vec(), vec(), vec(),
            _resident((LANES, WIDTH)), _resident((LANES, WIDTH)),
            _resident((LANES, WIDTH)), _resident((LANES, WIDTH)),
            _resident((WIDTH, WIDTH)), _resident((n, n)),
        ],
        out_specs=pl.BlockSpec((n, WIDTH), rmap(0)),
        out_shape=jax.ShapeDtypeStruct((batch * seq, WIDTH), BF16),
        scratch_shapes=[pltpu.VMEM((1, WIDTH), F32), pltpu.VMEM((1, WIDTH), F32),
                        pltpu.VMEM((1, WIDTH), F32), pltpu.VMEM((1, LANES), F32),
                        pltpu.VMEM((RW_HEADS // 2, LANES, LANES), F32)],
        compiler_params=_params("parallel", "arbitrary"),
        name="rwkv7",
    )(main, main, main, main, lora, *prm)


def _mem_kv_kernel(mem_ref, g_ref, w_ref, kg_ref, k_ref, v_ref):
    x = mem_ref[...]
    inv = lax.rsqrt(jnp.mean(x * x, axis=-1, keepdims=True) + NORM_EPS)
    h = (x * inv * g_ref[...]).astype(BF16)
    kv = _dot(h, w_ref[...])
    for hd in range(CA_HEADS):
        sl = slice(hd * CA_HEAD_DIM, (hd + 1) * CA_HEAD_DIM)
        kh = kv[:, sl]
        kh = kh * lax.rsqrt(jnp.mean(kh * kh, axis=-1, keepdims=True) + NORM_EPS) * kg_ref[...]
        k_ref[:, sl] = kh.astype(BF16)
    v_ref[...] = kv[:, WIDTH:].astype(BF16)


def _mem_kv(mem2, g, w, kg, batch, mem_len):
    row = lambda b: (b, 0)
    return pl.pallas_call(
        _mem_kv_kernel,
        grid=(batch,),
        in_specs=[pl.BlockSpec((mem_len, D_MODEL), row), _resident((1, D_MODEL)),
                  _resident((D_MODEL, 2 * WIDTH)), _resident((1, CA_HEAD_DIM))],
        out_specs=[pl.BlockSpec((mem_len, WIDTH), row), pl.BlockSpec((mem_len, WIDTH), row)],
        out_shape=[jax.ShapeDtypeStruct((batch * mem_len, WIDTH), BF16)] * 2,
        compiler_params=_params("parallel"),
        name="mem_kv",
    )(mem2, g, w, kg)


def _mem_attn_kernel(q_ref, z_ref, k_ref, v_ref, qg_ref, o_ref):
    scale = CA_HEAD_DIM ** -0.5
    for hd in range(CA_HEADS):
        sl = slice(hd * CA_HEAD_DIM, (hd + 1) * CA_HEAD_DIM)
        q = q_ref[:, sl].astype(F32)
        q = q * lax.rsqrt(jnp.mean(q * q, axis=-1, keepdims=True) + NORM_EPS) * (qg_ref[...] * scale)
        s = _dot_nt(q.astype(BF16), k_ref[:, sl])
        p = jnp.exp(s - jnp.max(s, axis=-1, keepdims=True))
        l = jnp.sum(p, axis=-1, keepdims=True)
        o = _dot(p.astype(BF16), v_ref[:, sl]) / l
        z = z_ref[:, sl].astype(F32)
        o_ref[:, sl] = (o * _silu(z)).astype(BF16)


def _mem_attn(main, km, vm, qg, batch, seq, mem_len):
    t = MEM_Q_TILE
    nq = seq // t
    return pl.pallas_call(
        _mem_attn_kernel,
        grid=(batch, nq),
        in_specs=[
            pl.BlockSpec((t, WIDTH), lambda b, i: (b * nq + i, COL_CA_Q)),
            pl.BlockSpec((t, WIDTH), lambda b, i: (b * nq + i, COL_CA_Z)),
            pl.BlockSpec((mem_len, WIDTH), lambda b, i: (b, 0)),
            pl.BlockSpec((mem_len, WIDTH), lambda b, i: (b, 0)),
            _resident((1, CA_HEAD_DIM)),
        ],
        out_specs=pl.BlockSpec((t, WIDTH), lambda b, i: (b * nq + i, 0)),
        out_shape=jax.ShapeDtypeStruct((batch * seq, WIDTH), BF16),
        compiler_params=_params("parallel", "parallel"),
        name="mem_attn",
    )(main, main, km, vm, qg)


def _merge_kernel(x_ref, ya_ref, yb_ref, yc_ref, ga_ref, gb_ref, gc_ref, wb_ref, wo_ref, o_ref):
    merged = None
    for n, (y_ref, g_ref) in enumerate(((ya_ref, ga_ref), (yb_ref, gb_ref), (yc_ref, gc_ref))):
        term = _sigmoid(g_ref[...].astype(F32)) * _dot(y_ref[...], wb_ref[n])
        merged = term if merged is None else merged + term
    o_ref[...] = x_ref[...] + _dot(merged.astype(BF16), wo_ref[...])


def _merge(x2, ya, yb, yc, main, wb, wo):
    tokens = x2.shape[0]
    tm = ROW_TILE
    row = lambda i: (i, 0)
    gate = lambda n: (lambda i: (i, COL_GATES // 2 + n))
    return pl.pallas_call(
        _merge_kernel,
        grid=(tokens // tm,),
        in_specs=[
            pl.BlockSpec((tm, D_MODEL), row),
            pl.BlockSpec((tm, WIDTH), row), pl.BlockSpec((tm, WIDTH), row),
            pl.BlockSpec((tm, WIDTH), row),
            pl.BlockSpec((tm, D_MODEL), gate(0)), pl.BlockSpec((tm, D_MODEL), gate(1)),
            pl.BlockSpec((tm, D_MODEL), gate(2)),
            _resident((3, WIDTH, D_MODEL)), _resident((D_MODEL, D_MODEL)),
        ],
        out_specs=pl.BlockSpec((tm, D_MODEL), row),
        out_shape=jax.ShapeDtypeStruct((tokens, D_MODEL), F32),
        compiler_params=_params("parallel"),
        name="merge",
    )(x2, ya, yb, yc, main, main, main, wb, wo)


def _rope_tables(seq):
    rot = 2 * ROPE_HALF
    inv = 1.0 / (ROPE_THETA ** (jnp.arange(0, rot, 2, dtype=F32) / rot))
    ang = jnp.arange(seq, dtype=F32)[:, None] * inv[None, :]
    cos, sin = jnp.cos(ang), jnp.sin(ang)
    pad = DA_QK_DIM - rot
    ones = jnp.ones((seq, pad), F32)
    zeros = jnp.zeros((seq, pad), F32)
    zh = jnp.zeros((seq, ROPE_HALF), F32)
    cosf = jnp.concatenate([cos, cos, ones], axis=1)
    sinm = jnp.concatenate([-sin, zh, zeros], axis=1)
    sinp = jnp.concatenate([zh, sin, zeros], axis=1)
    tile = lambda t: jnp.tile(t, (1, LANES // DA_QK_DIM))
    return tile(cosf), tile(sinm), tile(sinp)


def _block_diag_ones(width, block, dtype):
    i = jnp.arange(width) // block
    return (i[:, None] == i[None, :]).astype(dtype)


def _reorder_w_in(w):
    cols = lambda lo, n: w[:, lo:lo + n]
    da = cols(0, 2048)
    rw_rkv = cols(2048, 1536)
    lora = cols(3584, 2 * RW_LORA)
    rw_z = cols(3712, 512)
    rest = cols(4224, 1024 + 3072)
    return jnp.concatenate([da, rw_rkv, rw_z, rest], axis=1).astype(BF16), lora.astype(BF16)


def kernel(x, mem, norm_g, mem_norm_g, w_in, w_mem_kv, da_q_norm, da_k_norm, da_lambda, da_subln,
           rw_mu, rw_w0, rw_w_up, rw_a0, rw_a_up, rw_k_k, rw_k_a, rw_r_k, rw_ln_g, rw_ln_b,
           ca_q_norm, ca_k_norm, w_branch, w_out):
    batch, seq, _ = x.shape
    mem_len = mem.shape[1]
    depth = w_in.shape[0]
    assert seq % ROW_TILE == 0 and seq % ATT_TILE == 0 and seq % CHUNK == 0
    assert seq % MEM_Q_TILE == 0 and mem_len % 8 == 0

    cosf, sinm, sinp = _rope_tables(seq)
    bd64 = _block_diag_ones(WIDTH, DA_QK_DIM, BF16)
    tri = jnp.tril(jnp.ones((CHUNK, CHUNK), BF16))
    row = lambda t: t.reshape(1, -1).astype(F32)
    zpad = jnp.zeros((RW_LORA, WIDTH), F32)

    def hi_lo(w):
        hi = w.astype(BF16)
        return hi, (w - hi.astype(F32)).astype(BF16)

    x2 = x.reshape(batch * seq, D_MODEL)
    mem2 = mem.reshape(batch * mem_len, D_MODEL)
    for l in range(depth):
        lam_init = 0.8 - 0.6 * math.exp(-0.3 * l)
        w_main, w_lora = _reorder_w_in(w_in[l])
        qg = row(jnp.tile(da_q_norm[l], WIDTH // DA_QK_DIM)) * (DA_QK_DIM ** -0.5)
        kg = row(jnp.tile(da_k_norm[l], WIDTH // DA_QK_DIM))
        main, lora = _in_proj(x2, row(norm_g[l]), w_main, w_lora, qg, kg, cosf, sinm, sinp, bd64,
                              seq)

        ya = _diff_attn(main, da_lambda[l].astype(F32), row(da_subln[l]), batch, seq, lam_init)

        mu = rw_mu[l]
        wup_hi, wup_lo = hi_lo(jnp.concatenate([rw_w_up[l], zpad], axis=0))
        aup_hi, aup_lo = hi_lo(jnp.concatenate([zpad, rw_a_up[l]], axis=0))
        prm = (row(mu[:512]), row(mu[512:1024]), row(mu[1024:1536]), row(mu[1536:]),
               row(rw_w0[l]), row(rw_a0[l]), row(rw_k_k[l]), row(rw_k_a[l]), row(rw_r_k[l]),
               row(rw_ln_g[l]), row(rw_ln_b[l]), wup_hi, wup_lo, aup_hi, aup_lo, bd64, tri)
        yb = _rwkv(main, lora, prm, batch, seq)

        km, vm = _mem_kv(mem2, row(mem_norm_g[l]), w_mem_kv[l].astype(BF16), row(ca_k_norm[l]),
                         batch, mem_len)
        yc = _mem_attn(main, km, vm, row(ca_q_norm[l]), batch, seq, mem_len)

        x2 = _merge(x2, ya, yb, yc, main, w_branch[l].astype(BF16), w_out[l].astype(BF16))
    return x2.reshape(batch, seq, D_MODEL)
```

```python
import functools
import math

import jax
import jax.numpy as jnp
from jax import lax
from jax.experimental import pallas as pl
from jax.experimental.pallas import tpu as pltpu

F32 = jnp.float32
BF16 = jnp.bfloat16

D_MODEL = 1024
WIDTH = 512
DA_HEADS = 4
DA_QK_DIM = 64
DA_V_DIM = 128
RW_HEADS = 8
RW_HEAD = 64
RW_LORA = 64
CA_HEADS = 4
CA_HEAD_DIM = 128
ROPE_THETA = 500000.0
ROPE_HALF = DA_QK_DIM // 8
NORM_EPS = 1e-6
GN_EPS = 64e-5

LANES = 128
MAIN_WIDTH = 8192
VMEM_LIMIT = 56 * 1024 * 1024

COL_DA_Q, COL_DA_K, COL_DA_V, COL_DA_Z = 0, 1, 2, 3
COL_RW_R, COL_RW_K, COL_RW_V, COL_RW_Z = 4, 5, 6, 7
COL_CA_Q, COL_CA_Z = 8, 9
COL_GATES = 10

ROW_TILE = 512
ATT_TILE = 256
CHUNK = 64
RW_STEP_CHUNKS = 4
MEM_Q_TILE = 512


def _params(*sem):
    return pltpu.CompilerParams(dimension_semantics=sem, vmem_limit_bytes=VMEM_LIMIT)


def _resident(shape):
    nd = len(shape)
    return pl.BlockSpec(shape, lambda *_: (0,) * nd, pipeline_mode=pl.Buffered(1))


def _silu(z):
    return z * (1.0 / (1.0 + jnp.exp(-z)))


def _sigmoid(z):
    return 1.0 / (1.0 + jnp.exp(-z))


def _split3(x):
    hi = x.astype(BF16)
    r1 = x - hi.astype(F32)
    mid = r1.astype(BF16)
    lo = (r1 - mid.astype(F32)).astype(BF16)
    return hi, mid, lo


def _dot(a, b):
    return jnp.dot(a, b, preferred_element_type=F32)


def _dot_nt(a, b):
    return lax.dot_general(a, b, (((1,), (1,)), ((), ())), preferred_element_type=F32)


def _dot_tn(a, b):
    return lax.dot_general(a, b, (((0,), (0,)), ((), ())), preferred_element_type=F32)


def _in_proj_kernel(x_ref, g_ref, w_ref, wlo_ref, qg_ref, kg_ref, cos_ref, sinm_ref, sinp_ref,
                    bd_ref, main_ref, lora_ref):
    x = x_ref[...]
    inv = lax.rsqrt(jnp.mean(x * x, axis=-1, keepdims=True) + NORM_EPS)
    h = (x * inv * g_ref[...]).astype(BF16)
    cosf, sinm, sinp = cos_ref[...], sinm_ref[...], sinp_ref[...]
    for c in range(MAIN_WIDTH // WIDTH):
        acc = _dot(h, w_ref[:, c * WIDTH:(c + 1) * WIDTH])
        if c in (COL_DA_Q, COL_DA_K):
            gain = qg_ref[...] if c == COL_DA_Q else kg_ref[...]
            ssq = _dot((acc * acc).astype(BF16), bd_ref[...])
            acc = acc * lax.rsqrt(ssq * (1.0 / DA_QK_DIM) + NORM_EPS) * gain
            for s in range(WIDTH // LANES):
                t = acc[:, s * LANES:(s + 1) * LANES]
                t = (t * cosf + pltpu.roll(t, LANES - ROPE_HALF, 1) * sinm
                     + pltpu.roll(t, ROPE_HALF, 1) * sinp)
                lo = c * WIDTH + s * LANES
                main_ref[:, lo:lo + LANES] = t.astype(BF16)
        else:
            main_ref[:, c * WIDTH:(c + 1) * WIDTH] = acc.astype(BF16)
    lora_ref[...] = _dot(h, wlo_ref[...])


def _in_proj(x2, g, w_main, w_lora, qg, kg, cosf, sinm, sinp, bd, seq):
    tokens = x2.shape[0]
    tm = ROW_TILE
    pos_blocks = seq // tm
    row = lambda i: (i, 0)
    pos = lambda i: (i % pos_blocks, 0)
    return pl.pallas_call(
        _in_proj_kernel,
        grid=(tokens // tm,),
        in_specs=[
            pl.BlockSpec((tm, D_MODEL), row),
            _resident((1, D_MODEL)),
            _resident((D_MODEL, MAIN_WIDTH)),
            _resident((D_MODEL, LANES)),
            _resident((1, WIDTH)),
            _resident((1, WIDTH)),
            pl.BlockSpec((tm, LANES), pos),
            pl.BlockSpec((tm, LANES), pos),
            pl.BlockSpec((tm, LANES), pos),
            _resident((WIDTH, WIDTH)),
        ],
        out_specs=[pl.BlockSpec((tm, MAIN_WIDTH), row), pl.BlockSpec((tm, LANES), row)],
        out_shape=[jax.ShapeDtypeStruct((tokens, MAIN_WIDTH), BF16),
                   jax.ShapeDtypeStruct((tokens, LANES), F32)],
        compiler_params=_params("parallel"),
        name="in_proj",
    )(x2, g, w_main, w_lora, qg, kg, cosf, sinm, sinp, bd)


def _diff_attn_kernel(lam_ref, q_ref, k_ref, v_ref, z_ref, sub_ref, o_ref, qs_sc, m_sc, acc_sc,
                      *, lam_init):
    t = ATT_TILE
    i = pl.program_id(1)
    lane = lax.broadcasted_iota(jnp.int32, (t, LANES), 1)
    heads = tuple(slice(h * LANES, (h + 1) * LANES) for h in range(DA_HEADS))
    for h, sl in enumerate(heads):
        q = q_ref[:, sl]
        zero = jnp.zeros_like(q)
        qs_sc[h] = jnp.concatenate([jnp.where(lane < DA_QK_DIM, q, zero),
                                    jnp.where(lane >= DA_QK_DIM, q, zero)], axis=0)
    m_sc[...] = jnp.full(m_sc.shape, -jnp.inf, F32)
    acc_sc[...] = jnp.zeros(acc_sc.shape, F32)
    ones = jnp.ones((t, LANES), BF16)

    def scores(h, sl, start, masked):
        s = _dot_nt(qs_sc[h], k_ref[pl.ds(start, t), sl])
        if masked:
            r = lax.broadcasted_iota(jnp.int32, (2 * t, t), 0)
            r = jnp.where(r >= t, r - t, r)
            c = lax.broadcasted_iota(jnp.int32, (2 * t, t), 1)
            s = jnp.where(r >= c, s, -jnp.inf)
        return s

    def max_step(j, masked):
        start = pl.multiple_of(j * t, t)
        ss = [scores(h, sl, start, masked) for h, sl in enumerate(heads)]
        for h, s in enumerate(ss):
            m_sc[h] = jnp.maximum(m_sc[h], jnp.maximum(s[:, :LANES], s[:, LANES:]))

    def sum_step(j, masked):
        start = pl.multiple_of(j * t, t)
        ss = [scores(h, sl, start, masked) for h, sl in enumerate(heads)]
        ps = []
        for h, s in enumerate(ss):
            m = m_sc[h]
            ps.append(jnp.exp(s - jnp.concatenate([m, m], axis=1)).astype(BF16))
        for h, sl in enumerate(heads):
            v_ext = jnp.concatenate([v_ref[pl.ds(start, t), sl], ones], axis=1)
            acc_sc[h] += _dot(ps[h], v_ext)

    def loop(step):
        def body(j, carry):
            step(j, False)
            return carry
        lax.fori_loop(0, i, body, 0)
        step(i, True)

    loop(max_step)
    for h in range(DA_HEADS):
        m_sc[h] = jnp.broadcast_to(jnp.max(m_sc[h], axis=-1, keepdims=True), (2 * t, LANES))
    loop(sum_step)

    lv = lam_ref[...]
    lam = (jnp.exp(jnp.sum(lv[0:1] * lv[1:2], axis=-1, keepdims=True))
           - jnp.exp(jnp.sum(lv[2:3] * lv[3:4], axis=-1, keepdims=True)) + lam_init)
    for h, sl in enumerate(heads):
        acc = acc_sc[h]
        num, den = acc[:, :DA_V_DIM], acc[:, DA_V_DIM:]
        o = num[:t] / den[:t] - lam * (num[t:] / den[t:])
        o = o * lax.rsqrt(jnp.mean(o * o, axis=-1, keepdims=True) + NORM_EPS) * sub_ref[...]
        o = o * (1.0 - lam_init)
        z = z_ref[:, sl].astype(F32)
        o_ref[:, sl] = (o * _silu(z)).astype(BF16)


def _diff_attn(main, lam_vecs, subln, batch, seq, lam_init):
    t = ATT_TILE
    nq = seq // t
    return pl.pallas_call(
        functools.partial(_diff_attn_kernel, lam_init=lam_init),
        grid=(batch, nq),
        in_specs=[
            _resident((4, DA_QK_DIM)),
            pl.BlockSpec((t, WIDTH), lambda b, i: (b * nq + i, COL_DA_Q)),
            pl.BlockSpec((seq, WIDTH), lambda b, i: (b, COL_DA_K)),
            pl.BlockSpec((seq, WIDTH), lambda b, i: (b, COL_DA_V)),
            pl.BlockSpec((t, WIDTH), lambda b, i: (b * nq + i, COL_DA_Z)),
            _resident((1, DA_V_DIM)),
        ],
        out_specs=pl.BlockSpec((t, WIDTH), lambda b, i: (b * nq + i, 0)),
        out_shape=jax.ShapeDtypeStruct((batch * seq, WIDTH), BF16),
        scratch_shapes=[pltpu.VMEM((DA_HEADS, 2 * t, LANES), BF16),
                        pltpu.VMEM((DA_HEADS, 2 * t, LANES), F32),
                        pltpu.VMEM((DA_HEADS, 2 * t, 2 * DA_V_DIM), F32)],
        compiler_params=_params("parallel", "arbitrary"),
        name="diff_attn",
    )(lam_vecs, main, main, main, main, subln)


def _rwkv_kernel(r_ref, k_ref, v_ref, z_ref, lo_ref, mur_ref, muk_ref, muv_ref, mulo_ref,
                 w0_ref, a0_ref, kk_ref, ka_ref, rk_ref, lng_ref, lnb_ref,
                 wup_hi_ref, wup_lo_ref, aup_hi_ref, aup_lo_ref, bd_ref, tri_ref,
                 o_ref, cr_sc, ck_sc, cv_sc, clo_sc, state_sc):
    n = CHUNK
    rows = RW_STEP_CHUNKS * n

    @pl.when(pl.program_id(1) == 0)
    def _():
        cr_sc[...] = jnp.zeros(cr_sc.shape, F32)
        ck_sc[...] = jnp.zeros(ck_sc.shape, F32)
        cv_sc[...] = jnp.zeros(cv_sc.shape, F32)
        clo_sc[...] = jnp.zeros(clo_sc.shape, F32)
        state_sc[...] = jnp.zeros(state_sc.shape, F32)

    def shift(p, carry_ref, mu):
        first = lax.broadcasted_iota(jnp.int32, p.shape, 0) == 0
        prev = jnp.where(first, carry_ref[...], pltpu.roll(p, 1, 0))
        carry_ref[...] = p[rows - 1:rows, :]
        return p + (prev - p) * mu

    r = shift(r_ref[...].astype(F32), cr_sc, mur_ref[...])
    k = shift(k_ref[...].astype(F32), ck_sc, muk_ref[...])
    v = shift(v_ref[...].astype(F32), cv_sc, muv_ref[...])
    lo = shift(lo_ref[...], clo_sc, mulo_ref[...])

    def dot_hi(x, w_hi_ref, w_lo_ref):
        xh = x.astype(BF16)
        xl = (x - xh.astype(F32)).astype(BF16)
        w_hi = w_hi_ref[...]
        return _dot(xh, w_hi) + _dot(xl, w_hi) + _dot(xh, w_lo_ref[...])

    wlog = w0_ref[...] + dot_hi(jnp.tanh(lo), wup_hi_ref, wup_lo_ref)
    alog = a0_ref[...] + dot_hi(lo, aup_hi_ref, aup_lo_ref)
    tneg = -wlog
    softplus = jnp.maximum(tneg, 0.0) + jnp.log(1.0 + jnp.exp(-jnp.abs(tneg)))
    logdecay = -jnp.exp(-softplus - 0.5)
    a = _sigmoid(alog)

    bd = bd_ref[...]
    kk = k * kk_ref[...]
    kk2 = kk * kk
    kk2h = kk2.astype(BF16)
    kk2l = (kk2 - kk2h.astype(F32)).astype(BF16)
    ssq = _dot(kk2h, bd) + _dot(kk2l, bd)
    kk = kk / jnp.maximum(jnp.sqrt(ssq), 1e-12)
    kmod = k * (1.0 + (a - 1.0) * ka_ref[...])

    tri = tri_ref[...]
    d_hi, d_mid, d_lo = _split3(logdecay)
    cum = _dot(tri, d_hi) + _dot(tri, d_mid) + _dot(tri, d_lo)
    e_in = jnp.exp(cum)
    e_ex = jnp.exp(cum - logdecay)
    e_inv = jnp.exp(-cum)
    a_t = -kk * e_ex
    r_t = r * e_in
    b_t = kk * a * e_inv
    k_t = kmod * e_inv
    rkb = r * kmod * rk_ref[...]
    z = z_ref[...].astype(F32)
    lng, lnb = lng_ref[...], lnb_ref[...]

    lane = lax.broadcasted_iota(jnp.int32, (n, LANES), 1)
    first_head = lane < RW_HEAD

    def stack(xp):
        zero = jnp.zeros_like(xp)
        return jnp.concatenate([jnp.where(first_head, xp, zero),
                                jnp.where(first_head, zero, xp)], axis=0)

    ri = lax.broadcasted_iota(jnp.int32, (2 * n, 2 * n), 0)
    ci = lax.broadcasted_iota(jnp.int32, (2 * n, 2 * n), 1)
    ri = jnp.where(ri >= n, ri - n, ri)
    ci = jnp.where(ci >= n, ci - n, ci)
    strict = ri > ci
    incl = ri >= ci
    eye = (lax.broadcasted_iota(jnp.int32, (2 * n, 2 * n), 0)
           == lax.broadcasted_iota(jnp.int32, (2 * n, 2 * n), 1)).astype(F32)
    srow = lax.broadcasted_iota(jnp.int32, (2 * n, LANES), 0)
    slane = lax.broadcasted_iota(jnp.int32, (2 * n, LANES), 1)
    own = (srow < n) == (slane < RW_HEAD)

    chains = [(c, p) for c in range(RW_STEP_CHUNKS) for p in range(RW_HEADS // 2)]

    def operands(c, p):
        rs = slice(c * n, (c + 1) * n)
        sl = slice(p * LANES, (p + 1) * LANES)
        pe = e_in[(c + 1) * n - 1:(c + 1) * n, sl]
        b_s, k_s, v_s = stack(b_t[rs, sl]), stack(k_t[rs, sl]), stack(v[rs, sl])
        return dict(
            a_s=stack(a_t[rs, sl]).astype(BF16), r_s=stack(r_t[rs, sl]), v_s=v_s,
            v_sb=v_s.astype(BF16), pe=pe,
            bk_in=jnp.concatenate([b_s, k_s], axis=0).astype(BF16),
            bk_out=jnp.concatenate([b_s * pe, k_s * pe], axis=0).astype(BF16),
            bonus=jnp.sum(stack(rkb[rs, sl]), axis=-1, keepdims=True) * v_s)

    ops = [operands(c, p) for c, p in chains]
    for o in ops:
        m = _dot_nt(jnp.concatenate([o["a_s"], o["r_s"].astype(BF16)], axis=0), o["bk_in"])
        o["x"] = jnp.where(strict, m[:2 * n, :2 * n], 0.0)
        o["a_ak"] = jnp.where(strict, m[:2 * n, 2 * n:], 0.0).astype(BF16)
        o["t_rb"] = jnp.where(incl, m[2 * n:, :2 * n], 0.0).astype(BF16)
        o["t_rk"] = jnp.where(incl, m[2 * n:, 2 * n:], 0.0).astype(BF16)
        o["tinv"] = eye + o["x"]
    for _ in range(int(math.log2(n)) - 1):
        for o in ops:
            xb = o["x"].astype(BF16)
            o["x"] = _dot(xb, xb)
        for o in ops:
            o["tinv"] = o["tinv"] + _dot(o["tinv"].astype(BF16), o["x"].astype(BF16))
    for o in ops:
        o["tb"] = o["tinv"].astype(BF16)
        o["w"] = _dot(o["tb"], o["a_s"]).astype(BF16)
        o["av"] = _dot(o["a_ak"], o["v_sb"]).astype(BF16)
    for o in ops:
        u0 = _dot(o["tb"], o["av"]).astype(BF16)
        o["uv0"] = jnp.concatenate([u0, o["v_sb"]], axis=0)
        o["rw"] = (o["r_s"] + _dot(o["t_rb"], o["w"])).astype(BF16)
        o["wbk"] = _dot_tn(o["w"], o["bk_out"][:2 * n]).astype(BF16)
    for o in ops:
        o["y0"] = _dot(jnp.concatenate([o["t_rb"], o["t_rk"]], axis=1), o["uv0"])
        o["s0t"] = _dot_tn(o["uv0"], o["bk_out"])

    pairs = range(RW_HEADS // 2)
    gts = [state_sc[p] for p in pairs]
    for c in range(RW_STEP_CHUNKS):
        rs = slice(c * n, (c + 1) * n)
        ys = []
        for p in pairs:
            o = ops[c * len(pairs) + p]
            gtb = gts[p].astype(BF16)
            ys.append(_dot_nt(o["rw"], gtb) + o["y0"])
            gts[p] = gts[p] * o["pe"] + _dot(gtb, o["wbk"]) + o["s0t"]
        for p in pairs:
            sl = slice(p * LANES, (p + 1) * LANES)
            y = ys[p]
            mean = jnp.sum(y, axis=-1, keepdims=True) * (1.0 / RW_HEAD)
            d = jnp.where(own, y - mean, 0.0)
            var = jnp.sum(d * d, axis=-1, keepdims=True) * (1.0 / RW_HEAD)
            yn = d * lax.rsqrt(var + GN_EPS)
            yp = yn[:n] + yn[n:]
            bonus = ops[c * len(pairs) + p]["bonus"]
            bp = bonus[:n] + bonus[n:]
            out = (yp * lng[:, sl] + lnb[:, sl] + bp) * _silu(z[rs, sl])
            o_ref[rs, sl] = out.astype(BF16)
    for p in pairs:
        state_sc[p] = gts[p]


def _rwkv(main, lora, prm, batch, seq):
    n = RW_STEP_CHUNKS * CHUNK
    nc = seq // n
    rmap = lambda col: (lambda b, c: (b * nc + c, col))
    vec = lambda: _resident((1, WIDTH))
    return pl.pallas_call(
        _rwkv_kernel,
        grid=(batch, nc),
        in_specs=[
            pl.BlockSpec((n, WIDTH), rmap(COL_RW_R)),
            pl.BlockSpec((n, WIDTH), rmap(COL_RW_K)),
            pl.BlockSpec((n, WIDTH), rmap(COL_RW_V)),
            pl.BlockSpec((n, WIDTH), rmap(COL_RW_Z)),
            pl.BlockSpec((n, LANES), rmap(0)),
            vec(), vec(), vec(), _resident((1, LANES)),
            vec(), vec(), vec(), vec(), vec(), vec(), vec(),
            _resident((LANES, WIDTH)), _resident((LANES, WIDTH)),
            _resident((LANES, WIDTH)), _resident((LANES, WIDTH)),
            _resident((WIDTH, WIDTH)), _resident((n, n)),
        ],
        out_specs=pl.BlockSpec((n, WIDTH), rmap(0)),
        out_shape=jax.ShapeDtypeStruct((batch * seq, WIDTH), BF16),
        scratch_shapes=[pltpu.VMEM((1, WIDTH), F32), pltpu.VMEM((1, WIDTH), F32),
                        pltpu.VMEM((1, WIDTH), F32), pltpu.VMEM((1, LANES), F32),
                        pltpu.VMEM((RW_HEADS // 2, LANES, LANES), F32)],
        compiler_params=_params("parallel", "arbitrary"),
        name="rwkv7",
    )(main, main, main, main, lora, *prm)


def _mem_kv_kernel(mem_ref, g_ref, w_ref, kg_ref, k_ref, v_ref):
    x = mem_ref[...]
    inv = lax.rsqrt(jnp.mean(x * x, axis=-1, keepdims=True) + NORM_EPS)
    h = (x * inv * g_ref[...]).astype(BF16)
    kv = _dot(h, w_ref[...])
    for hd in range(CA_HEADS):
        sl = slice(hd * CA_HEAD_DIM, (hd + 1) * CA_HEAD_DIM)
        kh = kv[:, sl]
        kh = kh * lax.rsqrt(jnp.mean(kh * kh, axis=-1, keepdims=True) + NORM_EPS) * kg_ref[...]
        k_ref[:, sl] = kh.astype(BF16)
    v_ref[...] = kv[:, WIDTH:].astype(BF16)


def _mem_kv(mem2, g, w, kg, batch, mem_len):
    row = lambda b: (b, 0)
    return pl.pallas_call(
        _mem_kv_kernel,
        grid=(batch,),
        in_specs=[pl.BlockSpec((mem_len, D_MODEL), row), _resident((1, D_MODEL)),
                  _resident((D_MODEL, 2 * WIDTH)), _resident((1, CA_HEAD_DIM))],
        out_specs=[pl.BlockSpec((mem_len, WIDTH), row), pl.BlockSpec((mem_len, WIDTH), row)],
        out_shape=[jax.ShapeDtypeStruct((batch * mem_len, WIDTH), BF16)] * 2,
        compiler_params=_params("parallel"),
        name="mem_kv",
    )(mem2, g, w, kg)


def _mem_attn_kernel(q_ref, z_ref, k_ref, v_ref, qg_ref, o_ref):
    scale = CA_HEAD_DIM ** -0.5
    for hd in range(CA_HEADS):
        sl = slice(hd * CA_HEAD_DIM, (hd + 1) * CA_HEAD_DIM)
        q = q_ref[:, sl].astype(F32)
        q = q * lax.rsqrt(jnp.mean(q * q, axis=-1, keepdims=True) + NORM_EPS) * (qg_ref[...] * scale)
        s = _dot_nt(q.astype(BF16), k_ref[:, sl])
        p = jnp.exp(s - jnp.max(s, axis=-1, keepdims=True))
        l = jnp.sum(p, axis=-1, keepdims=True)
        o = _dot(p.astype(BF16), v_ref[:, sl]) / l
        z = z_ref[:, sl].astype(F32)
        o_ref[:, sl] = (o * _silu(z)).astype(BF16)


def _mem_attn(main, km, vm, qg, batch, seq, mem_len):
    t = MEM_Q_TILE
    nq = seq // t
    return pl.pallas_call(
        _mem_attn_kernel,
        grid=(batch, nq),
        in_specs=[
            pl.BlockSpec((t, WIDTH), lambda b, i: (b * nq + i, COL_CA_Q)),
            pl.BlockSpec((t, WIDTH), lambda b, i: (b * nq + i, COL_CA_Z)),
            pl.BlockSpec((mem_len, WIDTH), lambda b, i: (b, 0)),
            pl.BlockSpec((mem_len, WIDTH), lambda b, i: (b, 0)),
            _resident((1, CA_HEAD_DIM)),
        ],
        out_specs=pl.BlockSpec((t, WIDTH), lambda b, i: (b * nq + i, 0)),
        out_shape=jax.ShapeDtypeStruct((batch * seq, WIDTH), BF16),
        compiler_params=_params("parallel", "parallel"),
        name="mem_attn",
    )(main, main, km, vm, qg)


def _merge_kernel(x_ref, ya_ref, yb_ref, yc_ref, ga_ref, gb_ref, gc_ref, wb_ref, wo_ref, o_ref):
    merged = None
    for n, (y_ref, g_ref) in enumerate(((ya_ref, ga_ref), (yb_ref, gb_ref), (yc_ref, gc_ref))):
        term = _sigmoid(g_ref[...].astype(F32)) * _dot(y_ref[...], wb_ref[n])
        merged = term if merged is None else merged + term
    o_ref[...] = x_ref[...] + _dot(merged.astype(BF16), wo_ref[...])


def _merge(x2, ya, yb, yc, main, wb, wo):
    tokens = x2.shape[0]
    tm = ROW_TILE
    row = lambda i: (i, 0)
    gate = lambda n: (lambda i: (i, COL_GATES // 2 + n))
    return pl.pallas_call(
        _merge_kernel,
        grid=(tokens // tm,),
        in_specs=[
            pl.BlockSpec((tm, D_MODEL), row),
            pl.BlockSpec((tm, WIDTH), row), pl.BlockSpec((tm, WIDTH), row),
            pl.BlockSpec((tm, WIDTH), row),
            pl.BlockSpec((tm, D_MODEL), gate(0)), pl.BlockSpec((tm, D_MODEL), gate(1)),
            pl.BlockSpec((tm, D_MODEL), gate(2)),
            _resident((3, WIDTH, D_MODEL)), _resident((D_MODEL, D_MODEL)),
        ],
        out_specs=pl.BlockSpec((tm, D_MODEL), row),
        out_shape=jax.ShapeDtypeStruct((tokens, D_MODEL), F32),
        compiler_params=_params("parallel"),
        name="merge",
    )(x2, ya, yb, yc, main, main, main, wb, wo)


def _rope_tables(seq):
    rot = 2 * ROPE_HALF
    inv = 1.0 / (ROPE_THETA ** (jnp.arange(0, rot, 2, dtype=F32) / rot))
    ang = jnp.arange(seq, dtype=F32)[:, None] * inv[None, :]
    cos, sin = jnp.cos(ang), jnp.sin(ang)
    pad = DA_QK_DIM - rot
    ones = jnp.ones((seq, pad), F32)
    zeros = jnp.zeros((seq, pad), F32)
    zh = jnp.zeros((seq, ROPE_HALF), F32)
    cosf = jnp.concatenate([cos, cos, ones], axis=1)
    sinm = jnp.concatenate([-sin, zh, zeros], axis=1)
    sinp = jnp.concatenate([zh, sin, zeros], axis=1)
    tile = lambda t: jnp.tile(t, (1, LANES // DA_QK_DIM))
    return tile(cosf), tile(sinm), tile(sinp)


def _block_diag_ones(width, block, dtype):
    i = jnp.arange(width) // block
    return (i[:, None] == i[None, :]).astype(dtype)


def _chunk_tril(size, chunk):
    i = jnp.arange(size)
    same = (i[:, None] // chunk) == (i[None, :] // chunk)
    return (same & (i[None, :] <= i[:, None])).astype(BF16)


def _reorder_w_in(w):
    cols = lambda lo, n: w[:, lo:lo + n]
    da = cols(0, 2048)
    rw_rkv = cols(2048, 1536)
    lora = cols(3584, 2 * RW_LORA)
    rw_z = cols(3712, 512)
    rest = cols(4224, 1024 + 3072)
    return jnp.concatenate([da, rw_rkv, rw_z, rest], axis=1).astype(BF16), lora.astype(BF16)


def kernel(x, mem, norm_g, mem_norm_g, w_in, w_mem_kv, da_q_norm, da_k_norm, da_lambda, da_subln,
           rw_mu, rw_w0, rw_w_up, rw_a0, rw_a_up, rw_k_k, rw_k_a, rw_r_k, rw_ln_g, rw_ln_b,
           ca_q_norm, ca_k_norm, w_branch, w_out):
    batch, seq, _ = x.shape
    mem_len = mem.shape[1]
    depth = w_in.shape[0]
    assert seq % ROW_TILE == 0 and seq % ATT_TILE == 0 and seq % (RW_STEP_CHUNKS * CHUNK) == 0
    assert seq % MEM_Q_TILE == 0 and mem_len % 8 == 0

    cosf, sinm, sinp = _rope_tables(seq)
    bd64 = _block_diag_ones(WIDTH, DA_QK_DIM, BF16)
    tri = _chunk_tril(RW_STEP_CHUNKS * CHUNK, CHUNK)
    row = lambda t: t.reshape(1, -1).astype(F32)
    zpad = jnp.zeros((RW_LORA, WIDTH), F32)

    def hi_lo(w):
        hi = w.astype(BF16)
        return hi, (w - hi.astype(F32)).astype(BF16)

    x2 = x.reshape(batch * seq, D_MODEL)
    mem2 = mem.reshape(batch * mem_len, D_MODEL)
    for l in range(depth):
        lam_init = 0.8 - 0.6 * math.exp(-0.3 * l)
        w_main, w_lora = _reorder_w_in(w_in[l])
        qg = row(jnp.tile(da_q_norm[l], WIDTH // DA_QK_DIM)) * (DA_QK_DIM ** -0.5)
        kg = row(jnp.tile(da_k_norm[l], WIDTH // DA_QK_DIM))
        main, lora = _in_proj(x2, row(norm_g[l]), w_main, w_lora, qg, kg, cosf, sinm, sinp, bd64,
                              seq)

        ya = _diff_attn(main, da_lambda[l].astype(F32), row(da_subln[l]), batch, seq, lam_init)

        mu = rw_mu[l]
        wup_hi, wup_lo = hi_lo(jnp.concatenate([rw_w_up[l], zpad], axis=0))
        aup_hi, aup_lo = hi_lo(jnp.concatenate([zpad, rw_a_up[l]], axis=0))
        prm = (row(mu[:512]), row(mu[512:1024]), row(mu[1024:1536]), row(mu[1536:]),
               row(rw_w0[l]), row(rw_a0[l]), row(rw_k_k[l]), row(rw_k_a[l]), row(rw_r_k[l]),
               row(rw_ln_g[l]), row(rw_ln_b[l]), wup_hi, wup_lo, aup_hi, aup_lo, bd64, tri)
        yb = _rwkv(main, lora, prm, batch, seq)

        km, vm = _mem_kv(mem2, row(mem_norm_g[l]), w_mem_kv[l].astype(BF16), row(ca_k_norm[l]),
                         batch, mem_len)
        yc = _mem_attn(main, km, vm, row(ca_q_norm[l]), batch, seq, mem_len)

        x2 = _merge(x2, ya, yb, yc, main, w_branch[l].astype(BF16), w_out[l].astype(BF16))
    return x2.reshape(batch, seq, D_MODEL)
```

```python
import functools
import math

import jax
import jax.numpy as jnp
from jax import lax
from jax.experimental import pallas as pl
from jax.experimental.pallas import tpu as pltpu

F32 = jnp.float32
BF16 = jnp.bfloat16

D_MODEL = 1024
WIDTH = 512
DA_HEADS = 4
DA_QK_DIM = 64
DA_V_DIM = 128
RW_HEADS = 8
RW_HEAD = 64
RW_LORA = 64
CA_HEADS = 4
CA_HEAD_DIM = 128
ROPE_THETA = 500000.0
ROPE_HALF = DA_QK_DIM // 8
NORM_EPS = 1e-6
GN_EPS = 64e-5

LANES = 128
MAIN_WIDTH = 8192
VMEM_LIMIT = 56 * 1024 * 1024

COL_DA_Q, COL_DA_K, COL_DA_V, COL_DA_Z = 0, 1, 2, 3
COL_RW_R, COL_RW_K, COL_RW_V, COL_RW_Z = 4, 5, 6, 7
COL_CA_Q, COL_CA_Z = 8, 9
COL_GATES = 10

ROW_TILE = 512
ATT_TILE = 256
CHUNK = 64
RW_STEP_CHUNKS = 4
RW_SUBSTEPS = 2
SIDE_PIECES_PER_MAIN = 2
MEM_Q_TILE = 512


def _params(*sem):
    return pltpu.CompilerParams(dimension_semantics=sem, vmem_limit_bytes=VMEM_LIMIT)


def _resident(shape):
    nd = len(shape)
    return pl.BlockSpec(shape, lambda *_: (0,) * nd, pipeline_mode=pl.Buffered(1))


def _silu(z):
    return z * (1.0 / (1.0 + jnp.exp(-z)))


def _sigmoid(z):
    return 1.0 / (1.0 + jnp.exp(-z))


def _split3(x):
    hi = x.astype(BF16)
    r1 = x - hi.astype(F32)
    mid = r1.astype(BF16)
    lo = (r1 - mid.astype(F32)).astype(BF16)
    return hi, mid, lo


def _dot(a, b):
    return jnp.dot(a, b, preferred_element_type=F32)


def _dot_nt(a, b):
    return lax.dot_general(a, b, (((1,), (1,)), ((), ())), preferred_element_type=F32)


def _dot_tn(a, b):
    return lax.dot_general(a, b, (((0,), (0,)), ((), ())), preferred_element_type=F32)


def _in_proj_kernel(x_ref, g_ref, w_ref, wlo_ref, qg_ref, kg_ref, cos_ref, sinm_ref, sinp_ref,
                    bd_ref, main_ref, lora_ref):
    x = x_ref[...]
    inv = lax.rsqrt(jnp.mean(x * x, axis=-1, keepdims=True) + NORM_EPS)
    h = (x * inv * g_ref[...]).astype(BF16)
    cosf, sinm, sinp = cos_ref[...], sinm_ref[...], sinp_ref[...]
    for c in range(MAIN_WIDTH // WIDTH):
        acc = _dot(h, w_ref[:, c * WIDTH:(c + 1) * WIDTH])
        if c in (COL_DA_Q, COL_DA_K):
            gain = qg_ref[...] if c == COL_DA_Q else kg_ref[...]
            ssq = _dot((acc * acc).astype(BF16), bd_ref[...])
            acc = acc * lax.rsqrt(ssq * (1.0 / DA_QK_DIM) + NORM_EPS) * gain
            for s in range(WIDTH // LANES):
                t = acc[:, s * LANES:(s + 1) * LANES]
                t = (t * cosf + pltpu.roll(t, LANES - ROPE_HALF, 1) * sinm
                     + pltpu.roll(t, ROPE_HALF, 1) * sinp)
                lo = c * WIDTH + s * LANES
                main_ref[:, lo:lo + LANES] = t.astype(BF16)
        else:
            main_ref[:, c * WIDTH:(c + 1) * WIDTH] = acc.astype(BF16)
    lora_ref[...] = _dot(h, wlo_ref[...])


def _in_proj(x2, g, w_main, w_lora, qg, kg, cosf, sinm, sinp, bd, seq):
    tokens = x2.shape[0]
    tm = ROW_TILE
    pos_blocks = seq // tm
    row = lambda i: (i, 0)
    pos = lambda i: (i % pos_blocks, 0)
    return pl.pallas_call(
        _in_proj_kernel,
        grid=(tokens // tm,),
        in_specs=[
            pl.BlockSpec((tm, D_MODEL), row),
            _resident((1, D_MODEL)),
            _resident((D_MODEL, MAIN_WIDTH)),
            _resident((D_MODEL, LANES)),
            _resident((1, WIDTH)),
            _resident((1, WIDTH)),
            pl.BlockSpec((tm, LANES), pos),
            pl.BlockSpec((tm, LANES), pos),
            pl.BlockSpec((tm, LANES), pos),
            _resident((WIDTH, WIDTH)),
        ],
        out_specs=[pl.BlockSpec((tm, MAIN_WIDTH), row), pl.BlockSpec((tm, LANES), row)],
        out_shape=[jax.ShapeDtypeStruct((tokens, MAIN_WIDTH), BF16),
                   jax.ShapeDtypeStruct((tokens, LANES), F32)],
        compiler_params=_params("parallel"),
        name="in_proj",
    )(x2, g, w_main, w_lora, qg, kg, cosf, sinm, sinp, bd)


def _diff_attn_kernel(lam_ref, q_ref, k_ref, v_ref, z_ref, sub_ref, o_ref,
                      qs_sc, m_sc, l_sc, acc_sc, s_sc, *, lam_init):
    t = ATT_TILE
    i = pl.program_id(1)
    lane = lax.broadcasted_iota(jnp.int32, (t, LANES), 1)
    heads = tuple(slice(h * LANES, (h + 1) * LANES) for h in range(DA_HEADS))
    for h, sl in enumerate(heads):
        q = q_ref[:, sl]
        zero = jnp.zeros_like(q)
        qs_sc[h] = jnp.concatenate([jnp.where(lane < DA_QK_DIM, q, zero),
                                    jnp.where(lane >= DA_QK_DIM, q, zero)], axis=0)
    m_sc[...] = jnp.full(m_sc.shape, -jnp.inf, F32)
    l_sc[...] = jnp.zeros(l_sc.shape, F32)
    acc_sc[...] = jnp.zeros(acc_sc.shape, F32)

    def max_step(j, masked):
        start = pl.multiple_of(j * t, t)
        for h, sl in enumerate(heads):
            s = _dot_nt(qs_sc[h], k_ref[pl.ds(start, t), sl])
            if masked:
                r = lax.broadcasted_iota(jnp.int32, (2 * t, t), 0)
                r = jnp.where(r >= t, r - t, r)
                c = lax.broadcasted_iota(jnp.int32, (2 * t, t), 1)
                s = jnp.where(r >= c, s, -jnp.inf)
            s_sc[j, h] = s
            m_sc[h] = jnp.maximum(m_sc[h], jnp.maximum(s[:, :LANES], s[:, LANES:]))

    def sum_step(j, masked):
        del masked
        start = pl.multiple_of(j * t, t)
        ps = []
        for h in range(DA_HEADS):
            m = m_sc[h]
            p = jnp.exp(s_sc[j, h] - jnp.concatenate([m, m], axis=1))
            l_sc[h] += p[:, :LANES] + p[:, LANES:]
            ps.append(p.astype(BF16))
        for h, sl in enumerate(heads):
            acc_sc[h] += _dot(ps[h], v_ref[pl.ds(start, t), sl])

    def loop(step):
        def body(j, carry):
            step(j, False)
            return carry
        lax.fori_loop(0, i, body, 0)
        step(i, True)

    loop(max_step)
    for h in range(DA_HEADS):
        m_sc[h] = jnp.broadcast_to(jnp.max(m_sc[h], axis=-1, keepdims=True), (2 * t, LANES))
    loop(sum_step)

    lv = lam_ref[...]
    lam = (jnp.exp(jnp.sum(lv[0:1] * lv[1:2], axis=-1, keepdims=True))
           - jnp.exp(jnp.sum(lv[2:3] * lv[3:4], axis=-1, keepdims=True)) + lam_init)
    for h, sl in enumerate(heads):
        num = acc_sc[h]
        den = jnp.sum(l_sc[h], axis=-1, keepdims=True)
        o = num[:t] / den[:t] - lam * (num[t:] / den[t:])
        o = o * lax.rsqrt(jnp.mean(o * o, axis=-1, keepdims=True) + NORM_EPS) * sub_ref[...]
        o = o * (1.0 - lam_init)
        z = z_ref[:, sl].astype(F32)
        o_ref[:, sl] = (o * _silu(z)).astype(BF16)


def _diff_attn(main, lam_vecs, subln, batch, seq, lam_init):
    t = ATT_TILE
    nq = seq // t
    return pl.pallas_call(
        functools.partial(_diff_attn_kernel, lam_init=lam_init),
        grid=(batch, nq),
        in_specs=[
            _resident((4, DA_QK_DIM)),
            pl.BlockSpec((t, WIDTH), lambda b, i: (b * nq + i, COL_DA_Q)),
            pl.BlockSpec((seq, WIDTH), lambda b, i: (b, COL_DA_K)),
            pl.BlockSpec((seq, WIDTH), lambda b, i: (b, COL_DA_V)),
            pl.BlockSpec((t, WIDTH), lambda b, i: (b * nq + i, COL_DA_Z)),
            _resident((1, DA_V_DIM)),
        ],
        out_specs=pl.BlockSpec((t, WIDTH), lambda b, i: (b * nq + i, 0)),
        out_shape=jax.ShapeDtypeStruct((batch * seq, WIDTH), BF16),
        scratch_shapes=[pltpu.VMEM((DA_HEADS, 2 * t, LANES), BF16),
                        pltpu.VMEM((DA_HEADS, 2 * t, LANES), F32),
                        pltpu.VMEM((DA_HEADS, 2 * t, LANES), F32),
                        pltpu.VMEM((DA_HEADS, 2 * t, DA_V_DIM), F32),
                        pltpu.VMEM((nq, DA_HEADS, 2 * t, t), F32)],
        compiler_params=_params("parallel", "arbitrary"),
        name="diff_attn",
    )(lam_vecs, main, main, main, main, subln)


def _rwkv_kernel(r_ref, k_ref, v_ref, z_ref, lo_ref, mur_ref, muk_ref, muv_ref, mulo_ref,
                 w0_ref, a0_ref, kk_ref, ka_ref, rk_ref, lng_ref, lnb_ref,
                 wup_ref, aup_ref, bd_ref, tri_ref,
                 o_ref, cr_sc, ck_sc, cv_sc, clo_sc, state_sc):
    @pl.when(pl.program_id(1) == 0)
    def _():
        cr_sc[...] = jnp.zeros(cr_sc.shape, F32)
        ck_sc[...] = jnp.zeros(ck_sc.shape, F32)
        cv_sc[...] = jnp.zeros(cv_sc.shape, F32)
        clo_sc[...] = jnp.zeros(clo_sc.shape, F32)
        state_sc[...] = jnp.zeros(state_sc.shape, F32)

    gts = [state_sc[p] for p in range(RW_HEADS // 2)]
    _software_pipeline([
        _rwkv_rows(sub, gts, r_ref, k_ref, v_ref, z_ref, lo_ref, mur_ref, muk_ref, muv_ref,
                   mulo_ref, w0_ref, a0_ref, kk_ref, ka_ref, rk_ref, lng_ref, lnb_ref, wup_ref,
                   aup_ref, bd_ref, tri_ref, o_ref, cr_sc, ck_sc, cv_sc, clo_sc)
        for sub in range(RW_SUBSTEPS)])
    for p in range(RW_HEADS // 2):
        state_sc[p] = gts[p]


def _software_pipeline(gens):
    nxt = [next(g) for g in gens]

    def emit(s):
        nxt[s] = next(gens[s], None)

    def drain(s, phase):
        while 0 <= s < len(gens) and nxt[s] == phase:
            emit(s)

    drain(0, "prep")
    for s in range(len(gens)):
        while nxt[s] == "main":
            emit(s)
            for _ in range(SIDE_PIECES_PER_MAIN):
                if s + 1 < len(gens) and nxt[s + 1] == "prep":
                    emit(s + 1)
                if s >= 1 and nxt[s - 1] == "epi":
                    emit(s - 1)
        drain(s + 1, "prep")
        drain(s - 1, "epi")
    drain(len(gens) - 1, "epi")


def _rwkv_rows(sub, gts, r_ref, k_ref, v_ref, z_ref, lo_ref, mur_ref, muk_ref, muv_ref, mulo_ref,
               w0_ref, a0_ref, kk_ref, ka_ref, rk_ref, lng_ref, lnb_ref,
               wup_ref, aup_ref, bd_ref, tri_ref,
               o_ref, cr_sc, ck_sc, cv_sc, clo_sc):
    n = CHUNK
    rows = RW_STEP_CHUNKS * n
    blk = pl.ds(sub * rows, rows)
    yield "prep"

    def shift(p, carry_ref, mu):
        first = lax.broadcasted_iota(jnp.int32, p.shape, 0) == 0
        prev = jnp.where(first, carry_ref[...], pltpu.roll(p, 1, 0))
        carry_ref[...] = p[rows - 1:rows, :]
        return p + (prev - p) * mu

    r = shift(r_ref[blk, :].astype(F32), cr_sc, mur_ref[...])
    yield "prep"
    k = shift(k_ref[blk, :].astype(F32), ck_sc, muk_ref[...])
    yield "prep"
    v = shift(v_ref[blk, :].astype(F32), cv_sc, muv_ref[...])
    lo = shift(lo_ref[blk, :], clo_sc, mulo_ref[...])
    yield "prep"

    def dot_hi(x, w3_ref):
        xh = x.astype(BF16)
        xl = (x - xh.astype(F32)).astype(BF16)
        return _dot(jnp.concatenate([xh, xl, xh], axis=1), w3_ref[...])

    wlog = w0_ref[...] + dot_hi(jnp.tanh(lo), wup_ref)
    alog = a0_ref[...] + dot_hi(lo, aup_ref)
    yield "prep"
    tneg = -wlog
    softplus = jnp.maximum(tneg, 0.0) + jnp.log(1.0 + jnp.exp(-jnp.abs(tneg)))
    logdecay = -jnp.exp(-softplus - 0.5)
    a = _sigmoid(alog)
    yield "prep"

    bd = bd_ref[...]
    kk = k * kk_ref[...]
    kk2 = kk * kk
    kk2h = kk2.astype(BF16)
    kk2l = (kk2 - kk2h.astype(F32)).astype(BF16)
    yield "prep"
    ssq = _dot(jnp.concatenate([kk2h, kk2l], axis=1), bd)
    kk = kk / jnp.maximum(jnp.sqrt(ssq), 1e-12)
    kmod = k * (1.0 + (a - 1.0) * ka_ref[...])
    yield "prep"

    yield "prep"
    cum = _dot(tri_ref[...], jnp.concatenate(_split3(logdecay), axis=0))
    yield "prep"
    e_in = jnp.exp(cum)
    e_ex = jnp.exp(cum - logdecay)
    e_inv = jnp.exp(-cum)
    yield "prep"
    a_t = -kk * e_ex
    r_t = r * e_in
    yield "prep"
    b_t = kk * a * e_inv
    k_t = kmod * e_inv
    rkb = r * kmod * rk_ref[...]
    z = z_ref[blk, :].astype(F32)
    lng, lnb = lng_ref[...], lnb_ref[...]
    yield "prep"

    lane = lax.broadcasted_iota(jnp.int32, (n, LANES), 1)
    first_head = lane < RW_HEAD

    def stack(xp):
        zero = jnp.zeros_like(xp)
        return jnp.concatenate([jnp.where(first_head, xp, zero),
                                jnp.where(first_head, zero, xp)], axis=0)

    def per_head(col):
        return jnp.where(first_head, col[:n], col[n:])

    ti = lax.broadcasted_iota(jnp.int32, (n, LANES), 0)
    si = jnp.where(first_head, lane, lane - RW_HEAD)
    strict = ti > si
    incl = ti >= si
    eye = (ti == si).astype(F32)
    same_head = ((lax.broadcasted_iota(jnp.int32, (LANES, LANES), 0) < RW_HEAD)
                 == (lax.broadcasted_iota(jnp.int32, (LANES, LANES), 1) < RW_HEAD))

    chains = [(c, p) for c in range(RW_STEP_CHUNKS) for p in range(RW_HEADS // 2)]

    def operands(c, p):
        rs = slice(c * n, (c + 1) * n)
        sl = slice(p * LANES, (p + 1) * LANES)
        pe = e_in[(c + 1) * n - 1:(c + 1) * n, sl]
        b_p, k_p, v_p = b_t[rs, sl], k_t[rs, sl], v[rs, sl]
        v_pb = v_p.astype(BF16)
        return dict(
            ar=jnp.concatenate([a_t[rs, sl], r_t[rs, sl]], axis=0).astype(BF16),
            a_st=stack(a_t[rs, sl].astype(BF16)), r_p=r_t[rs, sl], v_pb=v_pb, v_st=stack(v_pb),
            pe=pe,
            bk_st=jnp.concatenate([stack(b_p.astype(BF16)), stack(k_p.astype(BF16))], axis=0),
            bk_out=jnp.concatenate([b_p * pe, k_p * pe], axis=0).astype(BF16),
            bonus=per_head(jnp.sum(stack(rkb[rs, sl]), axis=-1, keepdims=True)) * v_p)

    ops = []
    for c, p in chains:
        ops.append(operands(c, p))
        yield "prep" if len(ops) < len(chains) else "main"
    for o in ops:
        m = _dot_nt(o["ar"], o["bk_st"])
        o["x"] = jnp.where(strict, m[:n, :LANES], 0.0)
        o["a_ak"] = jnp.where(strict, m[:n, LANES:], 0.0).astype(BF16)
        o["t_rb"] = jnp.where(incl, m[n:, :LANES], 0.0).astype(BF16)
        o["t_rk"] = jnp.where(incl, m[n:, LANES:], 0.0).astype(BF16)
        o["tinv"] = eye + o["x"]
    yield "main"
    for _ in range(int(math.log2(n)) - 1):
        for o in ops:
            xb = o["x"].astype(BF16)
            o["x"] = _dot(xb, stack(xb))
        yield "main"
        for o in ops:
            o["tinv"] = o["tinv"] + _dot(o["tinv"].astype(BF16), stack(o["x"].astype(BF16)))
        yield "main"
    for o in ops:
        o["tb"] = o["tinv"].astype(BF16)
        o["w"] = _dot(o["tb"], o["a_st"]).astype(BF16)
        o["av"] = _dot(o["a_ak"], o["v_st"]).astype(BF16)
    yield "main"
    for o in ops:
        o["u0"] = _dot(o["tb"], stack(o["av"])).astype(BF16)
        o["rw"] = (o["r_p"] + _dot(o["t_rb"], stack(o["w"]))).astype(BF16)
        wbk = _dot_tn(o["w"], o["bk_out"][:n])
        o["wbk"] = jnp.where(same_head, wbk, 0.0).astype(BF16)
    yield "main"
    for o in ops:
        o["y0"] = _dot(jnp.concatenate([o["t_rb"], o["t_rk"]], axis=1),
                       jnp.concatenate([stack(o["u0"]), o["v_st"]], axis=0))
        s0t = _dot_tn(jnp.concatenate([o["u0"], o["v_pb"]], axis=0), o["bk_out"])
        o["s0t"] = jnp.where(same_head, s0t, 0.0)

    pairs = range(RW_HEADS // 2)
    for c in range(RW_STEP_CHUNKS):
        yield "main"
        for p in pairs:
            o = ops[c * len(pairs) + p]
            gtb = gts[p].astype(BF16)
            o["y"] = _dot_nt(o["rw"], gtb) + o["y0"]
            gts[p] = gts[p] * o["pe"] + _dot(gtb, o["wbk"]) + o["s0t"]
    for c in range(RW_STEP_CHUNKS):
        rs = slice(c * n, (c + 1) * n)
        for p in pairs:
            yield "epi"
            sl = slice(p * LANES, (p + 1) * LANES)
            y = ops[c * len(pairs) + p]["y"]
            mean = per_head(jnp.sum(stack(y), axis=-1, keepdims=True)) * (1.0 / RW_HEAD)
            d = y - mean
            var = per_head(jnp.sum(stack(d * d), axis=-1, keepdims=True)) * (1.0 / RW_HEAD)
            yn = d * lax.rsqrt(var + GN_EPS)
            out = yn * lng[:, sl] + lnb[:, sl] + ops[c * len(pairs) + p]["bonus"]
            o_ref[pl.ds(sub * rows + c * n, n), sl] = (out * _silu(z[rs, sl])).astype(BF16)


def _rwkv(main, lora, prm, batch, seq):
    n = RW_SUBSTEPS * RW_STEP_CHUNKS * CHUNK
    nc = seq // n
    rmap = lambda col: (lambda b, c: (b * nc + c, col))
    vec = lambda: _resident((1, WIDTH))
    return pl.pallas_call(
        _rwkv_kernel,
        grid=(batch, nc),
        in_specs=[
            pl.BlockSpec((n, WIDTH), rmap(COL_RW_R)),
            pl.BlockSpec((n, WIDTH), rmap(COL_RW_K)),
            pl.BlockSpec((n, WIDTH), rmap(COL_RW_V)),
            pl.BlockSpec((n, WIDTH), rmap(COL_RW_Z)),
            pl.BlockSpec((n, LANES), rmap(0)),
            vec(), vec(), vec(), _resident((1, LANES)),
            vec(), vec(), vec(), vec(), vec(), vec(), vec(),
            _resident((3 * LANES, WIDTH)), _resident((3 * LANES, WIDTH)),
            _resident((2 * WIDTH, WIDTH)),
            _resident((RW_STEP_CHUNKS * CHUNK, 3 * RW_STEP_CHUNKS * CHUNK)),
        ],
        out_specs=pl.BlockSpec((n, WIDTH), rmap(0)),
        out_shape=jax.ShapeDtypeStruct((batch * seq, WIDTH), BF16),
        scratch_shapes=[pltpu.VMEM((1, WIDTH), F32), pltpu.VMEM((1, WIDTH), F32),
                        pltpu.VMEM((1, WIDTH), F32), pltpu.VMEM((1, LANES), F32),
                        pltpu.VMEM((RW_HEADS // 2, LANES, LANES), F32)],
        compiler_params=_params("parallel", "arbitrary"),
        name="rwkv7",
    )(main, main, main, main, lora, *prm)


def _mem_kv_kernel(mem_ref, g_ref, w_ref, kg_ref, k_ref, v_ref):
    x = mem_ref[...]
    inv = lax.rsqrt(jnp.mean(x * x, axis=-1, keepdims=True) + NORM_EPS)
    h = (x * inv * g_ref[...]).astype(BF16)
    kv = _dot(h, w_ref[...])
    for hd in range(CA_HEADS):
        sl = slice(hd * CA_HEAD_DIM, (hd + 1) * CA_HEAD_DIM)
        kh = kv[:, sl]
        kh = kh * lax.rsqrt(jnp.mean(kh * kh, axis=-1, keepdims=True) + NORM_EPS) * kg_ref[...]
        k_ref[:, sl] = kh.astype(BF16)
    v_ref[...] = kv[:, WIDTH:].astype(BF16)


def _mem_kv(mem2, g, w, kg, batch, mem_len):
    row = lambda b: (b, 0)
    return pl.pallas_call(
        _mem_kv_kernel,
        grid=(batch,),
        in_specs=[pl.BlockSpec((mem_len, D_MODEL), row), _resident((1, D_MODEL)),
                  _resident((D_MODEL, 2 * WIDTH)), _resident((1, CA_HEAD_DIM))],
        out_specs=[pl.BlockSpec((mem_len, WIDTH), row), pl.BlockSpec((mem_len, WIDTH), row)],
        out_shape=[jax.ShapeDtypeStruct((batch * mem_len, WIDTH), BF16)] * 2,
        compiler_params=_params("parallel"),
        name="mem_kv",
    )(mem2, g, w, kg)


def _mem_attn_kernel(q_ref, z_ref, k_ref, v_ref, qg_ref, o_ref):
    scale = CA_HEAD_DIM ** -0.5
    for hd in range(CA_HEADS):
        sl = slice(hd * CA_HEAD_DIM, (hd + 1) * CA_HEAD_DIM)
        q = q_ref[:, sl].astype(F32)
        q = q * lax.rsqrt(jnp.mean(q * q, axis=-1, keepdims=True) + NORM_EPS) * (qg_ref[...] * scale)
        s = _dot_nt(q.astype(BF16), k_ref[:, sl])
        p = jnp.exp(s - jnp.max(s, axis=-1, keepdims=True))
        l = jnp.sum(p, axis=-1, keepdims=True)
        o = _dot(p.astype(BF16), v_ref[:, sl]) / l
        z = z_ref[:, sl].astype(F32)
        o_ref[:, sl] = (o * _silu(z)).astype(BF16)


def _mem_attn(main, km, vm, qg, batch, seq, mem_len):
    t = MEM_Q_TILE
    nq = seq // t
    return pl.pallas_call(
        _mem_attn_kernel,
        grid=(batch, nq),
        in_specs=[
            pl.BlockSpec((t, WIDTH), lambda b, i: (b * nq + i, COL_CA_Q)),
            pl.BlockSpec((t, WIDTH), lambda b, i: (b * nq + i, COL_CA_Z)),
            pl.BlockSpec((mem_len, WIDTH), lambda b, i: (b, 0)),
            pl.BlockSpec((mem_len, WIDTH), lambda b, i: (b, 0)),
            _resident((1, CA_HEAD_DIM)),
        ],
        out_specs=pl.BlockSpec((t, WIDTH), lambda b, i: (b * nq + i, 0)),
        out_shape=jax.ShapeDtypeStruct((batch * seq, WIDTH), BF16),
        compiler_params=_params("parallel", "parallel"),
        name="mem_attn",
    )(main, main, km, vm, qg)


def _merge_kernel(x_ref, ya_ref, yb_ref, yc_ref, ga_ref, gb_ref, gc_ref, wb_ref, wo_ref, o_ref):
    merged = None
    for n, (y_ref, g_ref) in enumerate(((ya_ref, ga_ref), (yb_ref, gb_ref), (yc_ref, gc_ref))):
        term = _sigmoid(g_ref[...].astype(F32)) * _dot(y_ref[...], wb_ref[n])
        merged = term if merged is None else merged + term
    o_ref[...] = x_ref[...] + _dot(merged.astype(BF16), wo_ref[...])


def _merge(x2, ya, yb, yc, main, wb, wo):
    tokens = x2.shape[0]
    tm = ROW_TILE
    row = lambda i: (i, 0)
    gate = lambda n: (lambda i: (i, COL_GATES // 2 + n))
    return pl.pallas_call(
        _merge_kernel,
        grid=(tokens // tm,),
        in_specs=[
            pl.BlockSpec((tm, D_MODEL), row),
            pl.BlockSpec((tm, WIDTH), row), pl.BlockSpec((tm, WIDTH), row),
            pl.BlockSpec((tm, WIDTH), row),
            pl.BlockSpec((tm, D_MODEL), gate(0)), pl.BlockSpec((tm, D_MODEL), gate(1)),
            pl.BlockSpec((tm, D_MODEL), gate(2)),
            _resident((3, WIDTH, D_MODEL)), _resident((D_MODEL, D_MODEL)),
        ],
        out_specs=pl.BlockSpec((tm, D_MODEL), row),
        out_shape=jax.ShapeDtypeStruct((tokens, D_MODEL), F32),
        compiler_params=_params("parallel"),
        name="merge",
    )(x2, ya, yb, yc, main, main, main, wb, wo)


def _rope_tables(seq):
    rot = 2 * ROPE_HALF
    inv = 1.0 / (ROPE_THETA ** (jnp.arange(0, rot, 2, dtype=F32) / rot))
    ang = jnp.arange(seq, dtype=F32)[:, None] * inv[None, :]
    cos, sin = jnp.cos(ang), jnp.sin(ang)
    pad = DA_QK_DIM - rot
    ones = jnp.ones((seq, pad), F32)
    zeros = jnp.zeros((seq, pad), F32)
    zh = jnp.zeros((seq, ROPE_HALF), F32)
    cosf = jnp.concatenate([cos, cos, ones], axis=1)
    sinm = jnp.concatenate([-sin, zh, zeros], axis=1)
    sinp = jnp.concatenate([zh, sin, zeros], axis=1)
    tile = lambda t: jnp.tile(t, (1, LANES // DA_QK_DIM))
    return tile(cosf), tile(sinm), tile(sinp)


def _block_diag_ones(width, block, dtype):
    i = jnp.arange(width) // block
    return (i[:, None] == i[None, :]).astype(dtype)


def _chunk_tril(size, chunk):
    i = jnp.arange(size)
    same = (i[:, None] // chunk) == (i[None, :] // chunk)
    return (same & (i[None, :] <= i[:, None])).astype(BF16)


def _reorder_w_in(w):
    cols = lambda lo, n: w[:, lo:lo + n]
    da = cols(0, 2048)
    rw_rkv = cols(2048, 1536)
    lora = cols(3584, 2 * RW_LORA)
    rw_z = cols(3712, 512)
    rest = cols(4224, 1024 + 3072)
    return jnp.concatenate([da, rw_rkv, rw_z, rest], axis=1).astype(BF16), lora.astype(BF16)


def _constants(seq):
    bd64 = _block_diag_ones(WIDTH, DA_QK_DIM, BF16)
    tri = _chunk_tril(RW_STEP_CHUNKS * CHUNK, CHUNK)
    return dict(rope=_rope_tables(seq), bd64=bd64, bd64x2=jnp.concatenate([bd64, bd64], axis=0),
                tri3=jnp.concatenate([tri, tri, tri], axis=1))


def _layer(x2, mem2, l, p, cst, batch, seq, mem_len):
    row = lambda t: t.reshape(1, -1).astype(F32)
    zpad = jnp.zeros((RW_LORA, WIDTH), F32)

    def hi_hi_lo(w):
        hi = w.astype(BF16)
        return jnp.concatenate([hi, hi, (w - hi.astype(F32)).astype(BF16)], axis=0)

    lam_init = 0.8 - 0.6 * math.exp(-0.3 * l)
    w_main, w_lora = _reorder_w_in(p["w_in"][l])
    qg = row(jnp.tile(p["da_q_norm"][l], WIDTH // DA_QK_DIM)) * (DA_QK_DIM ** -0.5)
    kg = row(jnp.tile(p["da_k_norm"][l], WIDTH // DA_QK_DIM))
    main, lora = _in_proj(x2, row(p["norm_g"][l]), w_main, w_lora, qg, kg, *cst["rope"],
                          cst["bd64"], seq)

    ya = _diff_attn(main, p["da_lambda"][l].astype(F32), row(p["da_subln"][l]), batch, seq,
                    lam_init)

    mu = p["rw_mu"][l]
    wup = hi_hi_lo(jnp.concatenate([p["rw_w_up"][l], zpad], axis=0))
    aup = hi_hi_lo(jnp.concatenate([zpad, p["rw_a_up"][l]], axis=0))
    prm = (row(mu[:512]), row(mu[512:1024]), row(mu[1024:1536]), row(mu[1536:]),
           row(p["rw_w0"][l]), row(p["rw_a0"][l]), row(p["rw_k_k"][l]), row(p["rw_k_a"][l]),
           row(p["rw_r_k"][l]), row(p["rw_ln_g"][l]), row(p["rw_ln_b"][l]), wup, aup,
           cst["bd64x2"], cst["tri3"])
    yb = _rwkv(main, lora, prm, batch, seq)

    km, vm = _mem_kv(mem2, row(p["mem_norm_g"][l]), p["w_mem_kv"][l].astype(BF16),
                     row(p["ca_k_norm"][l]), batch, mem_len)
    yc = _mem_attn(main, km, vm, row(p["ca_q_norm"][l]), batch, seq, mem_len)

    x2 = _merge(x2, ya, yb, yc, main, p["w_branch"][l].astype(BF16), p["w_out"][l].astype(BF16))
    return x2, ya, yb, yc


def kernel(x, mem, norm_g, mem_norm_g, w_in, w_mem_kv, da_q_norm, da_k_norm, da_lambda, da_subln,
           rw_mu, rw_w0, rw_w_up, rw_a0, rw_a_up, rw_k_k, rw_k_a, rw_r_k, rw_ln_g, rw_ln_b,
           ca_q_norm, ca_k_norm, w_branch, w_out):
    batch, seq, _ = x.shape
    mem_len = mem.shape[1]
    assert seq % ROW_TILE == 0 and seq % ATT_TILE == 0 and seq % MEM_Q_TILE == 0
    assert seq % (RW_SUBSTEPS * RW_STEP_CHUNKS * CHUNK) == 0 and mem_len % 8 == 0
    p = dict(norm_g=norm_g, mem_norm_g=mem_norm_g, w_in=w_in, w_mem_kv=w_mem_kv,
             da_q_norm=da_q_norm, da_k_norm=da_k_norm, da_lambda=da_lambda, da_subln=da_subln,
             rw_mu=rw_mu, rw_w0=rw_w0, rw_w_up=rw_w_up, rw_a0=rw_a0, rw_a_up=rw_a_up,
             rw_k_k=rw_k_k, rw_k_a=rw_k_a, rw_r_k=rw_r_k, rw_ln_g=rw_ln_g, rw_ln_b=rw_ln_b,
             ca_q_norm=ca_q_norm, ca_k_norm=ca_k_norm, w_branch=w_branch, w_out=w_out)
    cst = _constants(seq)
    x2 = x.reshape(batch * seq, D_MODEL)
    mem2 = mem.reshape(batch * mem_len, D_MODEL)
    for l in range(w_in.shape[0]):
        x2 = _layer(x2, mem2, l, p, cst, batch, seq, mem_len)[0]
    return x2.reshape(batch, seq, D_MODEL)
```

```python
import functools
import math

import jax
import jax.numpy as jnp
from jax import lax
from jax.experimental import pallas as pl
from jax.experimental.pallas import tpu as pltpu

F32 = jnp.float32
BF16 = jnp.bfloat16

D_MODEL = 1024
WIDTH = 512
DA_HEADS = 4
DA_QK_DIM = 64
DA_V_DIM = 128
RW_HEADS = 8
RW_HEAD = 64
RW_LORA = 64
CA_HEADS = 4
CA_HEAD_DIM = 128
ROPE_THETA = 500000.0
ROPE_HALF = DA_QK_DIM // 8
NORM_EPS = 1e-6
LOG2_E = 1.4426950408889634
GN_EPS = 64e-5

LANES = 128
MAIN_WIDTH = 8192
VMEM_LIMIT = 56 * 1024 * 1024

COL_DA_Q, COL_DA_K, COL_DA_V, COL_DA_Z = 0, 1, 2, 3
COL_RW_R, COL_RW_K, COL_RW_V, COL_RW_Z = 4, 5, 6, 7
COL_CA_Q, COL_CA_Z = 8, 9
COL_GATES = 10

ROW_TILE = 512
ATT_TILE = 256
CHUNK = 64
RW_STEP_CHUNKS = 4
RW_SUBSTEPS = 2
SIDE_PIECES_PER_MAIN = 2
MEM_Q_TILE = 512


def _params(*sem):
    return pltpu.CompilerParams(dimension_semantics=sem, vmem_limit_bytes=VMEM_LIMIT)


def _resident(shape):
    nd = len(shape)
    return pl.BlockSpec(shape, lambda *_: (0,) * nd, pipeline_mode=pl.Buffered(1))


def _silu(z):
    return z * (1.0 / (1.0 + jnp.exp(-z)))


def _sigmoid(z):
    return 1.0 / (1.0 + jnp.exp(-z))


def _split3(x):
    hi = x.astype(BF16)
    r1 = x - hi.astype(F32)
    mid = r1.astype(BF16)
    lo = (r1 - mid.astype(F32)).astype(BF16)
    return hi, mid, lo


def _dot(a, b):
    return jnp.dot(a, b, preferred_element_type=F32)


def _dot_nt(a, b):
    return lax.dot_general(a, b, (((1,), (1,)), ((), ())), preferred_element_type=F32)


def _dot_tn(a, b):
    return lax.dot_general(a, b, (((0,), (0,)), ((), ())), preferred_element_type=F32)


def _in_proj_kernel(x_ref, g_ref, w_ref, wlo_ref, qg_ref, kg_ref, cos_ref, sinm_ref, sinp_ref,
                    bd_ref, main_ref, lora_ref):
    x = x_ref[...]
    inv = lax.rsqrt(jnp.mean(x * x, axis=-1, keepdims=True) + NORM_EPS)
    h = (x * inv * g_ref[...]).astype(BF16)
    cosf, sinm, sinp = cos_ref[...], sinm_ref[...], sinp_ref[...]
    for c in range(MAIN_WIDTH // WIDTH):
        acc = _dot(h, w_ref[:, c * WIDTH:(c + 1) * WIDTH])
        if c in (COL_DA_Q, COL_DA_K):
            gain = qg_ref[...] if c == COL_DA_Q else kg_ref[...]
            ssq = _dot((acc * acc).astype(BF16), bd_ref[...])
            acc = acc * lax.rsqrt(ssq * (1.0 / DA_QK_DIM) + NORM_EPS) * gain
            for s in range(WIDTH // LANES):
                t = acc[:, s * LANES:(s + 1) * LANES]
                t = (t * cosf + pltpu.roll(t, LANES - ROPE_HALF, 1) * sinm
                     + pltpu.roll(t, ROPE_HALF, 1) * sinp)
                lo = c * WIDTH + s * LANES
                main_ref[:, lo:lo + LANES] = t.astype(BF16)
        else:
            main_ref[:, c * WIDTH:(c + 1) * WIDTH] = acc.astype(BF16)
    lora_ref[...] = _dot(h, wlo_ref[...])


def _in_proj(x2, g, w_main, w_lora, qg, kg, cosf, sinm, sinp, bd, seq):
    tokens = x2.shape[0]
    tm = ROW_TILE
    pos_blocks = seq // tm
    row = lambda i: (i, 0)
    pos = lambda i: (i % pos_blocks, 0)
    return pl.pallas_call(
        _in_proj_kernel,
        grid=(tokens // tm,),
        in_specs=[
            pl.BlockSpec((tm, D_MODEL), row),
            _resident((1, D_MODEL)),
            _resident((D_MODEL, MAIN_WIDTH)),
            _resident((D_MODEL, LANES)),
            _resident((1, WIDTH)),
            _resident((1, WIDTH)),
            pl.BlockSpec((tm, LANES), pos),
            pl.BlockSpec((tm, LANES), pos),
            pl.BlockSpec((tm, LANES), pos),
            _resident((WIDTH, WIDTH)),
        ],
        out_specs=[pl.BlockSpec((tm, MAIN_WIDTH), row), pl.BlockSpec((tm, LANES), row)],
        out_shape=[jax.ShapeDtypeStruct((tokens, MAIN_WIDTH), BF16),
                   jax.ShapeDtypeStruct((tokens, LANES), F32)],
        compiler_params=_params("parallel"),
        name="in_proj",
    )(x2, g, w_main, w_lora, qg, kg, cosf, sinm, sinp, bd)


def _diff_attn_kernel(lam_ref, q_ref, k_ref, v_ref, z_ref, sub_ref, o_ref,
                      qs_sc, m_sc, l_sc, acc_sc, s_sc, *, lam_init):
    t = ATT_TILE
    i = pl.program_id(1)
    lane = lax.broadcasted_iota(jnp.int32, (t, LANES), 1)
    heads = tuple(slice(h * LANES, (h + 1) * LANES) for h in range(DA_HEADS))
    for h, sl in enumerate(heads):
        q = q_ref[:, sl]
        zero = jnp.zeros_like(q)
        qs_sc[h] = jnp.concatenate([jnp.where(lane < DA_QK_DIM, q, zero),
                                    jnp.where(lane >= DA_QK_DIM, q, zero)], axis=0)
    def max_step(j, diagonal):
        start = pl.multiple_of(j * t, t)
        for h, sl in enumerate(heads):
            s = _dot_nt(qs_sc[h], k_ref[pl.ds(start, t), sl])
            if diagonal:
                r = lax.broadcasted_iota(jnp.int32, (2 * t, t), 0)
                r = jnp.where(r >= t, r - t, r)
                c = lax.broadcasted_iota(jnp.int32, (2 * t, t), 1)
                s = jnp.where(r >= c, s, -jnp.inf)
            s_sc[j, h] = s
            fold = jnp.maximum(s[:, :LANES], s[:, LANES:])
            m_sc[h] = fold if diagonal else jnp.maximum(m_sc[h], fold)

    def sum_step(j, diagonal):
        start = pl.multiple_of(j * t, t)
        ps = []
        for h in range(DA_HEADS):
            m = m_sc[h]
            p = jnp.exp2(s_sc[j, h] - jnp.concatenate([m, m], axis=1))
            part = p[:, :LANES] + p[:, LANES:]
            l_sc[h] = part if diagonal else l_sc[h] + part
            ps.append(p.astype(BF16))
        for h, sl in enumerate(heads):
            pv = _dot(ps[h], v_ref[pl.ds(start, t), sl])
            acc_sc[h] = pv if diagonal else acc_sc[h] + pv

    def loop(step):
        def body(j, carry):
            step(j, False)
            return carry
        step(i, True)
        lax.fori_loop(0, i, body, 0)

    loop(max_step)
    for h in range(DA_HEADS):
        m_sc[h] = jnp.broadcast_to(jnp.max(m_sc[h], axis=-1, keepdims=True), (2 * t, LANES))
    loop(sum_step)

    lv = lam_ref[...]
    lam = (jnp.exp(jnp.sum(lv[0:1] * lv[1:2], axis=-1, keepdims=True))
           - jnp.exp(jnp.sum(lv[2:3] * lv[3:4], axis=-1, keepdims=True)) + lam_init)
    for h, sl in enumerate(heads):
        num = acc_sc[h]
        den = jnp.sum(l_sc[h], axis=-1, keepdims=True)
        o = num[:t] / den[:t] - lam * (num[t:] / den[t:])
        o = o * lax.rsqrt(jnp.mean(o * o, axis=-1, keepdims=True) + NORM_EPS) * sub_ref[...]
        o = o * (1.0 - lam_init)
        z = z_ref[:, sl].astype(F32)
        o_ref[:, sl] = (o * _silu(z)).astype(BF16)


def _diff_attn(main, lam_vecs, subln, batch, seq, lam_init):
    t = ATT_TILE
    nq = seq // t
    return pl.pallas_call(
        functools.partial(_diff_attn_kernel, lam_init=lam_init),
        grid=(batch, nq),
        in_specs=[
            _resident((4, DA_QK_DIM)),
            pl.BlockSpec((t, WIDTH), lambda b, i: (b * nq + i, COL_DA_Q)),
            pl.BlockSpec((seq, WIDTH), lambda b, i: (b, COL_DA_K)),
            pl.BlockSpec((seq, WIDTH), lambda b, i: (b, COL_DA_V)),
            pl.BlockSpec((t, WIDTH), lambda b, i: (b * nq + i, COL_DA_Z)),
            _resident((1, DA_V_DIM)),
        ],
        out_specs=pl.BlockSpec((t, WIDTH), lambda b, i: (b * nq + i, 0)),
        out_shape=jax.ShapeDtypeStruct((batch * seq, WIDTH), BF16),
        scratch_shapes=[pltpu.VMEM((DA_HEADS, 2 * t, LANES), BF16),
                        pltpu.VMEM((DA_HEADS, 2 * t, LANES), F32),
                        pltpu.VMEM((DA_HEADS, 2 * t, LANES), F32),
                        pltpu.VMEM((DA_HEADS, 2 * t, DA_V_DIM), F32),
                        pltpu.VMEM((nq, DA_HEADS, 2 * t, t), F32)],
        compiler_params=_params("parallel", "arbitrary"),
        name="diff_attn",
    )(lam_vecs, main, main, main, main, subln)


def _rwkv_kernel(r_ref, k_ref, v_ref, z_ref, lo_ref, mur_ref, muk_ref, muv_ref, mulo_ref,
                 w0_ref, a0_ref, kk_ref, ka_ref, rk_ref, lng_ref, lnb_ref,
                 wup_ref, aup_ref, bd_ref, tri_ref,
                 o_ref, cr_sc, ck_sc, cv_sc, clo_sc, state_sc):
    @pl.when(pl.program_id(1) == 0)
    def _():
        cr_sc[...] = jnp.zeros(cr_sc.shape, F32)
        ck_sc[...] = jnp.zeros(ck_sc.shape, F32)
        cv_sc[...] = jnp.zeros(cv_sc.shape, F32)
        clo_sc[...] = jnp.zeros(clo_sc.shape, F32)
        state_sc[...] = jnp.zeros(state_sc.shape, F32)

    gts = [state_sc[p] for p in range(RW_HEADS // 2)]
    _software_pipeline([
        _rwkv_rows(sub, gts, r_ref, k_ref, v_ref, z_ref, lo_ref, mur_ref, muk_ref, muv_ref,
                   mulo_ref, w0_ref, a0_ref, kk_ref, ka_ref, rk_ref, lng_ref, lnb_ref, wup_ref,
                   aup_ref, bd_ref, tri_ref, o_ref, cr_sc, ck_sc, cv_sc, clo_sc)
        for sub in range(RW_SUBSTEPS)])
    for p in range(RW_HEADS // 2):
        state_sc[p] = gts[p]


def _software_pipeline(gens):
    nxt = [next(g) for g in gens]

    def emit(s):
        nxt[s] = next(gens[s], None)

    def drain(s, phase):
        while 0 <= s < len(gens) and nxt[s] == phase:
            emit(s)

    drain(0, "prep")
    for s in range(len(gens)):
        while nxt[s] == "main":
            emit(s)
            for _ in range(SIDE_PIECES_PER_MAIN):
                if s + 1 < len(gens) and nxt[s + 1] == "prep":
                    emit(s + 1)
                if s >= 1 and nxt[s - 1] == "tail":
                    emit(s - 1)
        drain(s + 1, "prep")
        drain(s - 1, "tail")
    drain(len(gens) - 1, "tail")


def _rwkv_rows(sub, gts, r_ref, k_ref, v_ref, z_ref, lo_ref, mur_ref, muk_ref, muv_ref, mulo_ref,
               w0_ref, a0_ref, kk_ref, ka_ref, rk_ref, lng_ref, lnb_ref,
               wup_ref, aup_ref, bd_ref, tri_ref,
               o_ref, cr_sc, ck_sc, cv_sc, clo_sc):
    n = CHUNK
    rows = RW_STEP_CHUNKS * n
    blk = pl.ds(sub * rows, rows)
    yield "prep"

    def shift(p, carry_ref, mu):
        first = lax.broadcasted_iota(jnp.int32, p.shape, 0) == 0
        prev = jnp.where(first, carry_ref[...], pltpu.roll(p, 1, 0))
        carry_ref[...] = p[rows - 1:rows, :]
        return p + (prev - p) * mu

    r = shift(r_ref[blk, :].astype(F32), cr_sc, mur_ref[...])
    yield "prep"
    k = shift(k_ref[blk, :].astype(F32), ck_sc, muk_ref[...])
    yield "prep"
    v = shift(v_ref[blk, :].astype(F32), cv_sc, muv_ref[...])
    lo = shift(lo_ref[blk, :], clo_sc, mulo_ref[...])
    yield "prep"

    def dot_hi(x, w3_ref):
        xh = x.astype(BF16)
        xl = (x - xh.astype(F32)).astype(BF16)
        return _dot(jnp.concatenate([xh, xl, xh], axis=1), w3_ref[...])

    wlog = w0_ref[...] + dot_hi(jnp.tanh(lo), wup_ref)
    alog = a0_ref[...] + dot_hi(lo, aup_ref)
    yield "prep"
    tneg = -wlog
    softplus = jnp.maximum(tneg, 0.0) + jnp.log(1.0 + jnp.exp(-jnp.abs(tneg)))
    logdecay = -jnp.exp(-softplus - 0.5)
    a = _sigmoid(alog)
    yield "prep"

    bd = bd_ref[...]
    kk = k * kk_ref[...]
    kk2 = kk * kk
    kk2h = kk2.astype(BF16)
    kk2l = (kk2 - kk2h.astype(F32)).astype(BF16)
    yield "prep"
    ssq = _dot(jnp.concatenate([kk2h, kk2l], axis=1), bd)
    kk = kk / jnp.maximum(jnp.sqrt(ssq), 1e-12)
    kmod = k * (1.0 + (a - 1.0) * ka_ref[...])
    yield "prep"

    yield "prep"
    cum = _dot(tri_ref[...], jnp.concatenate(_split3(logdecay), axis=0))
    yield "prep"
    e_in = jnp.exp(cum)
    e_ex = jnp.exp(cum - logdecay)
    e_inv = jnp.exp(-cum)
    yield "prep"
    a_t = -kk * e_ex
    r_t = r * e_in
    yield "prep"
    b_t = kk * a * e_inv
    k_t = kmod * e_inv
    rkb = r * kmod * rk_ref[...]
    z = z_ref[blk, :].astype(F32)
    lng, lnb = lng_ref[...], lnb_ref[...]
    yield "prep"

    lane = lax.broadcasted_iota(jnp.int32, (n, LANES), 1)
    first_head = lane < RW_HEAD

    def stack(xp):
        zero = jnp.zeros_like(xp)
        return jnp.concatenate([jnp.where(first_head, xp, zero),
                                jnp.where(first_head, zero, xp)], axis=0)

    def per_head(col):
        return jnp.where(first_head, col[:n], col[n:])

    ti = lax.broadcasted_iota(jnp.int32, (n, LANES), 0)
    si = jnp.where(first_head, lane, lane - RW_HEAD)
    strict = ti > si
    incl = ti >= si
    eye = (ti == si).astype(F32)
    same_head = ((lax.broadcasted_iota(jnp.int32, (LANES, LANES), 0) < RW_HEAD)
                 == (lax.broadcasted_iota(jnp.int32, (LANES, LANES), 1) < RW_HEAD))

    chains = [(c, p) for c in range(RW_STEP_CHUNKS) for p in range(RW_HEADS // 2)]

    def operands(c, p):
        rs = slice(c * n, (c + 1) * n)
        sl = slice(p * LANES, (p + 1) * LANES)
        pe = e_in[(c + 1) * n - 1:(c + 1) * n, sl]
        b_p, k_p, v_p = b_t[rs, sl], k_t[rs, sl], v[rs, sl]
        v_pb = v_p.astype(BF16)
        return dict(
            ar=jnp.concatenate([a_t[rs, sl], r_t[rs, sl]], axis=0).astype(BF16),
            a_st=stack(a_t[rs, sl].astype(BF16)), r_p=r_t[rs, sl], v_pb=v_pb, v_st=stack(v_pb),
            pe=pe,
            bk_st=jnp.concatenate([stack(b_p.astype(BF16)), stack(k_p.astype(BF16))], axis=0),
            bk_out=jnp.concatenate([b_p * pe, k_p * pe], axis=0).astype(BF16),
            bonus=per_head(jnp.sum(stack(rkb[rs, sl]), axis=-1, keepdims=True)) * v_p)

    ops = []
    for c, p in chains:
        ops.append(operands(c, p))
        yield "prep" if len(ops) < len(chains) else "main"
    for o in ops:
        m = _dot_nt(o["ar"], o["bk_st"])
        o["x"] = jnp.where(strict, m[:n, :LANES], 0.0)
        o["a_ak"] = jnp.where(strict, m[:n, LANES:], 0.0).astype(BF16)
        o["t_rb"] = jnp.where(incl, m[n:, :LANES], 0.0).astype(BF16)
        o["t_rk"] = jnp.where(incl, m[n:, LANES:], 0.0).astype(BF16)
        o["tinv"] = eye + o["x"]
    yield "main"
    levels = int(math.log2(n)) - 1
    for o in ops:
        xb = o["x"].astype(BF16)
        o["x"] = _dot(xb, stack(xb))
    yield "main"
    for level in range(levels):
        for o in ops:
            xb = o["x"].astype(BF16)
            if level + 1 < levels:
                both = _dot(jnp.concatenate([o["tinv"].astype(BF16), xb], axis=0), stack(xb))
                o["tinv"] = o["tinv"] + both[:n]
                o["x"] = both[n:]
            else:
                o["tinv"] = o["tinv"] + _dot(o["tinv"].astype(BF16), stack(xb))
        yield "main"
    for o in ops:
        o["tb"] = o["tinv"].astype(BF16)
        o["w"] = _dot(o["tb"], o["a_st"]).astype(BF16)
        o["av"] = _dot(o["a_ak"], o["v_st"]).astype(BF16)
    yield "main"
    for o in ops:
        o["u0"] = _dot(o["tb"], stack(o["av"])).astype(BF16)
        o["rw"] = (o["r_p"] + _dot(o["t_rb"], stack(o["w"]))).astype(BF16)
        wbk = _dot_tn(o["w"], o["bk_out"][:n])
        o["wbk"] = jnp.where(same_head, wbk, 0.0).astype(BF16)
    yield "main"
    for o in ops:
        o["y0"] = _dot(jnp.concatenate([o["t_rb"], o["t_rk"]], axis=1),
                       jnp.concatenate([stack(o["u0"]), o["v_st"]], axis=0))
        s0t = _dot_tn(jnp.concatenate([o["u0"], o["v_pb"]], axis=0), o["bk_out"])
        o["s0t"] = jnp.where(same_head, s0t, 0.0)

    pairs = range(RW_HEADS // 2)
    for c in range(RW_STEP_CHUNKS):
        yield "tail"
        for p in pairs:
            o = ops[c * len(pairs) + p]
            gtb = gts[p].astype(BF16)
            o["y"] = _dot_nt(o["rw"], gtb) + o["y0"]
            gts[p] = gts[p] * o["pe"] + _dot(gtb, o["wbk"]) + o["s0t"]
    for c in range(RW_STEP_CHUNKS):
        rs = slice(c * n, (c + 1) * n)
        for p in pairs:
            yield "tail"
            sl = slice(p * LANES, (p + 1) * LANES)
            y = ops[c * len(pairs) + p]["y"]
            mean = per_head(jnp.sum(stack(y), axis=-1, keepdims=True)) * (1.0 / RW_HEAD)
            d = y - mean
            var = per_head(jnp.sum(stack(d * d), axis=-1, keepdims=True)) * (1.0 / RW_HEAD)
            yn = d * lax.rsqrt(var + GN_EPS)
            out = yn * lng[:, sl] + lnb[:, sl] + ops[c * len(pairs) + p]["bonus"]
            o_ref[pl.ds(sub * rows + c * n, n), sl] = (out * _silu(z[rs, sl])).astype(BF16)


def _rwkv(main, lora, prm, batch, seq):
    n = RW_SUBSTEPS * RW_STEP_CHUNKS * CHUNK
    nc = seq // n
    rmap = lambda col: (lambda b, c: (b * nc + c, col))
    vec = lambda: _resident((1, WIDTH))
    return pl.pallas_call(
        _rwkv_kernel,
        grid=(batch, nc),
        in_specs=[
            pl.BlockSpec((n, WIDTH), rmap(COL_RW_R)),
            pl.BlockSpec((n, WIDTH), rmap(COL_RW_K)),
            pl.BlockSpec((n, WIDTH), rmap(COL_RW_V)),
            pl.BlockSpec((n, WIDTH), rmap(COL_RW_Z)),
            pl.BlockSpec((n, LANES), rmap(0)),
            vec(), vec(), vec(), _resident((1, LANES)),
            vec(), vec(), vec(), vec(), vec(), vec(), vec(),
            _resident((3 * LANES, WIDTH)), _resident((3 * LANES, WIDTH)),
            _resident((2 * WIDTH, WIDTH)),
            _resident((RW_STEP_CHUNKS * CHUNK, 3 * RW_STEP_CHUNKS * CHUNK)),
        ],
        out_specs=pl.BlockSpec((n, WIDTH), rmap(0)),
        out_shape=jax.ShapeDtypeStruct((batch * seq, WIDTH), BF16),
        scratch_shapes=[pltpu.VMEM((1, WIDTH), F32), pltpu.VMEM((1, WIDTH), F32),
                        pltpu.VMEM((1, WIDTH), F32), pltpu.VMEM((1, LANES), F32),
                        pltpu.VMEM((RW_HEADS // 2, LANES, LANES), F32)],
        compiler_params=_params("parallel", "arbitrary"),
        name="rwkv7",
    )(main, main, main, main, lora, *prm)


def _mem_kv_kernel(mem_ref, g_ref, w_ref, kg_ref, k_ref, v_ref):
    x = mem_ref[...]
    inv = lax.rsqrt(jnp.mean(x * x, axis=-1, keepdims=True) + NORM_EPS)
    h = (x * inv * g_ref[...]).astype(BF16)
    kv = _dot(h, w_ref[...])
    for hd in range(CA_HEADS):
        sl = slice(hd * CA_HEAD_DIM, (hd + 1) * CA_HEAD_DIM)
        kh = kv[:, sl]
        kh = kh * lax.rsqrt(jnp.mean(kh * kh, axis=-1, keepdims=True) + NORM_EPS) * kg_ref[...]
        k_ref[:, sl] = kh.astype(BF16)
    v_ref[...] = kv[:, WIDTH:].astype(BF16)


def _mem_kv(mem2, g, w, kg, batch, mem_len):
    row = lambda b: (b, 0)
    return pl.pallas_call(
        _mem_kv_kernel,
        grid=(batch,),
        in_specs=[pl.BlockSpec((mem_len, D_MODEL), row), _resident((1, D_MODEL)),
                  _resident((D_MODEL, 2 * WIDTH)), _resident((1, CA_HEAD_DIM))],
        out_specs=[pl.BlockSpec((mem_len, WIDTH), row), pl.BlockSpec((mem_len, WIDTH), row)],
        out_shape=[jax.ShapeDtypeStruct((batch * mem_len, WIDTH), BF16)] * 2,
        compiler_params=_params("parallel"),
        name="mem_kv",
    )(mem2, g, w, kg)


def _mem_attn_kernel(q_ref, z_ref, k_ref, v_ref, qg_ref, o_ref):
    gain = qg_ref[...] * (CA_HEAD_DIM ** -0.5 * LOG2_E)
    heads = tuple(slice(hd * CA_HEAD_DIM, (hd + 1) * CA_HEAD_DIM) for hd in range(CA_HEADS))
    qs = []
    for sl in heads:
        q = q_ref[:, sl].astype(F32)
        q = q * lax.rsqrt(jnp.mean(q * q, axis=-1, keepdims=True) + NORM_EPS) * gain
        qs.append(q.astype(BF16))
    ss = [_dot_nt(q, k_ref[:, sl]) for q, sl in zip(qs, heads)]
    ps = [jnp.exp2(s - jnp.max(s, axis=-1, keepdims=True)) for s in ss]
    ls = [jnp.sum(p, axis=-1, keepdims=True) for p in ps]
    os_ = [_dot(p.astype(BF16), v_ref[:, sl]) for p, sl in zip(ps, heads)]
    for o, l, sl in zip(os_, ls, heads):
        z = z_ref[:, sl].astype(F32)
        o_ref[:, sl] = (o / l * _silu(z)).astype(BF16)


def _mem_attn(main, km, vm, qg, batch, seq, mem_len):
    t = MEM_Q_TILE
    nq = seq // t
    return pl.pallas_call(
        _mem_attn_kernel,
        grid=(batch, nq),
        in_specs=[
            pl.BlockSpec((t, WIDTH), lambda b, i: (b * nq + i, COL_CA_Q)),
            pl.BlockSpec((t, WIDTH), lambda b, i: (b * nq + i, COL_CA_Z)),
            pl.BlockSpec((mem_len, WIDTH), lambda b, i: (b, 0)),
            pl.BlockSpec((mem_len, WIDTH), lambda b, i: (b, 0)),
            _resident((1, CA_HEAD_DIM)),
        ],
        out_specs=pl.BlockSpec((t, WIDTH), lambda b, i: (b * nq + i, 0)),
        out_shape=jax.ShapeDtypeStruct((batch * seq, WIDTH), BF16),
        compiler_params=_params("parallel", "parallel"),
        name="mem_attn",
    )(main, main, km, vm, qg)


def _merge_kernel(x_ref, ya_ref, yb_ref, yc_ref, ga_ref, gb_ref, gc_ref, wb_ref, wo_ref, o_ref):
    merged = None
    for n, (y_ref, g_ref) in enumerate(((ya_ref, ga_ref), (yb_ref, gb_ref), (yc_ref, gc_ref))):
        term = _sigmoid(g_ref[...].astype(F32)) * _dot(y_ref[...], wb_ref[n])
        merged = term if merged is None else merged + term
    o_ref[...] = x_ref[...] + _dot(merged.astype(BF16), wo_ref[...])


def _merge(x2, ya, yb, yc, main, wb, wo):
    tokens = x2.shape[0]
    tm = ROW_TILE
    row = lambda i: (i, 0)
    gate = lambda n: (lambda i: (i, COL_GATES // 2 + n))
    return pl.pallas_call(
        _merge_kernel,
        grid=(tokens // tm,),
        in_specs=[
            pl.BlockSpec((tm, D_MODEL), row),
            pl.BlockSpec((tm, WIDTH), row), pl.BlockSpec((tm, WIDTH), row),
            pl.BlockSpec((tm, WIDTH), row),
            pl.BlockSpec((tm, D_MODEL), gate(0)), pl.BlockSpec((tm, D_MODEL), gate(1)),
            pl.BlockSpec((tm, D_MODEL), gate(2)),
            _resident((3, WIDTH, D_MODEL)), _resident((D_MODEL, D_MODEL)),
        ],
        out_specs=pl.BlockSpec((tm, D_MODEL), row),
        out_shape=jax.ShapeDtypeStruct((tokens, D_MODEL), F32),
        compiler_params=_params("parallel"),
        name="merge",
    )(x2, ya, yb, yc, main, main, main, wb, wo)


def _rope_tables(seq):
    rot = 2 * ROPE_HALF
    inv = 1.0 / (ROPE_THETA ** (jnp.arange(0, rot, 2, dtype=F32) / rot))
    ang = jnp.arange(seq, dtype=F32)[:, None] * inv[None, :]
    cos, sin = jnp.cos(ang), jnp.sin(ang)
    pad = DA_QK_DIM - rot
    ones = jnp.ones((seq, pad), F32)
    zeros = jnp.zeros((seq, pad), F32)
    zh = jnp.zeros((seq, ROPE_HALF), F32)
    cosf = jnp.concatenate([cos, cos, ones], axis=1)
    sinm = jnp.concatenate([-sin, zh, zeros], axis=1)
    sinp = jnp.concatenate([zh, sin, zeros], axis=1)
    tile = lambda t: jnp.tile(t, (1, LANES // DA_QK_DIM))
    return tile(cosf), tile(sinm), tile(sinp)


def _block_diag_ones(width, block, dtype):
    i = jnp.arange(width) // block
    return (i[:, None] == i[None, :]).astype(dtype)


def _chunk_tril(size, chunk):
    i = jnp.arange(size)
    same = (i[:, None] // chunk) == (i[None, :] // chunk)
    return (same & (i[None, :] <= i[:, None])).astype(BF16)


def _reorder_w_in(w):
    cols = lambda lo, n: w[:, lo:lo + n]
    da = cols(0, 2048)
    rw_rkv = cols(2048, 1536)
    lora = cols(3584, 2 * RW_LORA)
    rw_z = cols(3712, 512)
    rest = cols(4224, 1024 + 3072)
    return jnp.concatenate([da, rw_rkv, rw_z, rest], axis=1).astype(BF16), lora.astype(BF16)


def _constants(seq):
    bd64 = _block_diag_ones(WIDTH, DA_QK_DIM, BF16)
    tri = _chunk_tril(RW_STEP_CHUNKS * CHUNK, CHUNK)
    return dict(rope=_rope_tables(seq), bd64=bd64, bd64x2=jnp.concatenate([bd64, bd64], axis=0),
                tri3=jnp.concatenate([tri, tri, tri], axis=1))


def _layer(x2, mem2, l, p, cst, batch, seq, mem_len):
    row = lambda t: t.reshape(1, -1).astype(F32)
    zpad = jnp.zeros((RW_LORA, WIDTH), F32)

    def hi_hi_lo(w):
        hi = w.astype(BF16)
        return jnp.concatenate([hi, hi, (w - hi.astype(F32)).astype(BF16)], axis=0)

    lam_init = 0.8 - 0.6 * math.exp(-0.3 * l)
    w_main, w_lora = _reorder_w_in(p["w_in"][l])
    qg = row(jnp.tile(p["da_q_norm"][l], WIDTH // DA_QK_DIM)) * (DA_QK_DIM ** -0.5 * LOG2_E)
    kg = row(jnp.tile(p["da_k_norm"][l], WIDTH // DA_QK_DIM))
    main, lora = _in_proj(x2, row(p["norm_g"][l]), w_main, w_lora, qg, kg, *cst["rope"],
                          cst["bd64"], seq)

    ya = _diff_attn(main, p["da_lambda"][l].astype(F32), row(p["da_subln"][l]), batch, seq,
                    lam_init)

    mu = p["rw_mu"][l]
    wup = hi_hi_lo(jnp.concatenate([p["rw_w_up"][l], zpad], axis=0))
    aup = hi_hi_lo(jnp.concatenate([zpad, p["rw_a_up"][l]], axis=0))
    prm = (row(mu[:512]), row(mu[512:1024]), row(mu[1024:1536]), row(mu[1536:]),
           row(p["rw_w0"][l]), row(p["rw_a0"][l]), row(p["rw_k_k"][l]), row(p["rw_k_a"][l]),
           row(p["rw_r_k"][l]), row(p["rw_ln_g"][l]), row(p["rw_ln_b"][l]), wup, aup,
           cst["bd64x2"], cst["tri3"])
    yb = _rwkv(main, lora, prm, batch, seq)

    km, vm = _mem_kv(mem2, row(p["mem_norm_g"][l]), p["w_mem_kv"][l].astype(BF16),
                     row(p["ca_k_norm"][l]), batch, mem_len)
    yc = _mem_attn(main, km, vm, row(p["ca_q_norm"][l]), batch, seq, mem_len)

    x2 = _merge(x2, ya, yb, yc, main, p["w_branch"][l].astype(BF16), p["w_out"][l].astype(BF16))
    return x2, ya, yb, yc


def kernel(x, mem, norm_g, mem_norm_g, w_in, w_mem_kv, da_q_norm, da_k_norm, da_lambda, da_subln,
           rw_mu, rw_w0, rw_w_up, rw_a0, rw_a_up, rw_k_k, rw_k_a, rw_r_k, rw_ln_g, rw_ln_b,
           ca_q_norm, ca_k_norm, w_branch, w_out):
    batch, seq, _ = x.shape
    mem_len = mem.shape[1]
    assert seq % ROW_TILE == 0 and seq % ATT_TILE == 0 and seq % MEM_Q_TILE == 0
    assert seq % (RW_SUBSTEPS * RW_STEP_CHUNKS * CHUNK) == 0 and mem_len % 8 == 0
    p = dict(norm_g=norm_g, mem_norm_g=mem_norm_g, w_in=w_in, w_mem_kv=w_mem_kv,
             da_q_norm=da_q_norm, da_k_norm=da_k_norm, da_lambda=da_lambda, da_subln=da_subln,
             rw_mu=rw_mu, rw_w0=rw_w0, rw_w_up=rw_w_up, rw_a0=rw_a0, rw_a_up=rw_a_up,
             rw_k_k=rw_k_k, rw_k_a=rw_k_a, rw_r_k=rw_r_k, rw_ln_g=rw_ln_g, rw_ln_b=rw_ln_b,
             ca_q_norm=ca_q_norm, ca_k_norm=ca_k_norm, w_branch=w_branch, w_out=w_out)
    cst = _constants(seq)
    x2 = x.reshape(batch * seq, D_MODEL)
    mem2 = mem.reshape(batch * mem_len, D_MODEL)
    for l in range(w_in.shape[0]):
        x2 = _layer(x2, mem2, l, p, cst, batch, seq, mem_len)[0]
    return x2.reshape(batch, seq, D_MODEL)
```

```python
import functools
import math

import jax
import jax.numpy as jnp
from jax import lax
from jax.experimental import pallas as pl
from jax.experimental.pallas import tpu as pltpu

F32 = jnp.float32
BF16 = jnp.bfloat16

D_MODEL = 1024
WIDTH = 512
DA_HEADS = 4
DA_QK_DIM = 64
DA_V_DIM = 128
RW_HEADS = 8
RW_HEAD = 64
RW_LORA = 64
CA_HEADS = 4
CA_HEAD_DIM = 128
ROPE_THETA = 500000.0
ROPE_HALF = DA_QK_DIM // 8
NORM_EPS = 1e-6
LOG2_E = 1.4426950408889634
GN_EPS = 64e-5

LANES = 128
MAIN_WIDTH = 8192
VMEM_LIMIT = 56 * 1024 * 1024

COL_DA_Q, COL_DA_K, COL_DA_V, COL_DA_Z = 0, 1, 2, 3
COL_RW_R, COL_RW_K, COL_RW_V, COL_RW_Z = 4, 5, 6, 7
COL_CA_Q, COL_CA_Z = 8, 9
COL_GATES = 10

ROW_TILE = 512
ATT_TILE = 256
CHUNK = 64
RW_STEP_CHUNKS = 4
RW_SUBSTEPS = 2
SIDE_PIECES_PER_MAIN = 2
MEM_Q_TILE = 512


def _params(*sem):
    return pltpu.CompilerParams(dimension_semantics=sem, vmem_limit_bytes=VMEM_LIMIT)


def _resident(shape):
    nd = len(shape)
    return pl.BlockSpec(shape, lambda *_: (0,) * nd, pipeline_mode=pl.Buffered(1))


def _silu(z):
    return z * (1.0 / (1.0 + jnp.exp(-z)))


def _sigmoid(z):
    return 1.0 / (1.0 + jnp.exp(-z))


def _split3(x):
    hi = x.astype(BF16)
    r1 = x - hi.astype(F32)
    mid = r1.astype(BF16)
    lo = (r1 - mid.astype(F32)).astype(BF16)
    return hi, mid, lo


def _dot(a, b):
    return jnp.dot(a, b, preferred_element_type=F32)


def _dot_nt(a, b):
    return lax.dot_general(a, b, (((1,), (1,)), ((), ())), preferred_element_type=F32)


def _dot_tn(a, b):
    return lax.dot_general(a, b, (((0,), (0,)), ((), ())), preferred_element_type=F32)


def _in_proj_kernel(x_ref, g_ref, w_ref, wlo_ref, qg_ref, kg_ref, cos_ref, sinm_ref, sinp_ref,
                    bd_ref, main_ref, lora_ref):
    x = x_ref[...]
    inv = lax.rsqrt(jnp.mean(x * x, axis=-1, keepdims=True) + NORM_EPS)
    h = (x * inv * g_ref[...]).astype(BF16)
    cosf, sinm, sinp = cos_ref[...], sinm_ref[...], sinp_ref[...]
    for c in range(MAIN_WIDTH // WIDTH):
        acc = _dot(h, w_ref[:, c * WIDTH:(c + 1) * WIDTH])
        if c in (COL_DA_Q, COL_DA_K):
            gain = qg_ref[...] if c == COL_DA_Q else kg_ref[...]
            ssq = _dot((acc * acc).astype(BF16), bd_ref[...])
            acc = acc * lax.rsqrt(ssq * (1.0 / DA_QK_DIM) + NORM_EPS) * gain
            for s in range(WIDTH // LANES):
                t = acc[:, s * LANES:(s + 1) * LANES]
                t = (t * cosf + pltpu.roll(t, LANES - ROPE_HALF, 1) * sinm
                     + pltpu.roll(t, ROPE_HALF, 1) * sinp)
                lo = c * WIDTH + s * LANES
                main_ref[:, lo:lo + LANES] = t.astype(BF16)
        else:
            main_ref[:, c * WIDTH:(c + 1) * WIDTH] = acc.astype(BF16)
    lora_ref[...] = _dot(h, wlo_ref[...])


def _in_proj(x2, g, w_main, w_lora, qg, kg, cosf, sinm, sinp, bd, seq):
    tokens = x2.shape[0]
    tm = ROW_TILE
    pos_blocks = seq // tm
    row = lambda i: (i, 0)
    pos = lambda i: (i % pos_blocks, 0)
    return pl.pallas_call(
        _in_proj_kernel,
        grid=(tokens // tm,),
        in_specs=[
            pl.BlockSpec((tm, D_MODEL), row),
            _resident((1, D_MODEL)),
            _resident((D_MODEL, MAIN_WIDTH)),
            _resident((D_MODEL, LANES)),
            _resident((1, WIDTH)),
            _resident((1, WIDTH)),
            pl.BlockSpec((tm, LANES), pos),
            pl.BlockSpec((tm, LANES), pos),
            pl.BlockSpec((tm, LANES), pos),
            _resident((WIDTH, WIDTH)),
        ],
        out_specs=[pl.BlockSpec((tm, MAIN_WIDTH), row), pl.BlockSpec((tm, LANES), row)],
        out_shape=[jax.ShapeDtypeStruct((tokens, MAIN_WIDTH), BF16),
                   jax.ShapeDtypeStruct((tokens, LANES), F32)],
        compiler_params=_params("parallel"),
        name="in_proj",
    )(x2, g, w_main, w_lora, qg, kg, cosf, sinm, sinp, bd)


def _diff_attn_kernel(lam_ref, q_ref, k_ref, v_ref, z_ref, sub_ref, o_ref,
                      qs_sc, m_sc, l_sc, acc_sc, s_sc, *, lam_init):
    t = ATT_TILE
    i = pl.program_id(1)
    lane = lax.broadcasted_iota(jnp.int32, (t, LANES), 1)
    heads = tuple(slice(h * LANES, (h + 1) * LANES) for h in range(DA_HEADS))
    for h, sl in enumerate(heads):
        q = q_ref[:, sl]
        zero = jnp.zeros_like(q)
        qs_sc[h] = jnp.concatenate([jnp.where(lane < DA_QK_DIM, q, zero),
                                    jnp.where(lane >= DA_QK_DIM, q, zero)], axis=0)
    def max_step(j0, nb, diagonal):
        ss = {}
        for b in range(nb):
            start = pl.multiple_of((j0 + b) * t, t)
            for h, sl in enumerate(heads):
                s = _dot_nt(qs_sc[h], k_ref[pl.ds(start, t), sl])
                if diagonal:
                    r = lax.broadcasted_iota(jnp.int32, (2 * t, t), 0)
                    r = jnp.where(r >= t, r - t, r)
                    c = lax.broadcasted_iota(jnp.int32, (2 * t, t), 1)
                    s = jnp.where(r >= c, s, -jnp.inf)
                ss[b, h] = s
        for h in range(DA_HEADS):
            fold = None
            for b in range(nb):
                s = ss[b, h]
                s_sc[j0 + b, h] = s
                fb = jnp.maximum(s[:, :LANES], s[:, LANES:])
                fold = fb if fold is None else jnp.maximum(fold, fb)
            m_sc[h] = fold if diagonal else jnp.maximum(m_sc[h], fold)

    def sum_step(j0, nb, diagonal):
        start = pl.multiple_of(j0 * t, t)
        ps = []
        for h in range(DA_HEADS):
            m = m_sc[h]
            mm = jnp.concatenate([m, m], axis=1)
            part, pb = None, []
            for b in range(nb):
                p = jnp.exp2(s_sc[j0 + b, h] - mm)
                pp = p[:, :LANES] + p[:, LANES:]
                part = pp if part is None else part + pp
                pb.append(p.astype(BF16))
            l_sc[h] = part if diagonal else l_sc[h] + part
            ps.append(pb[0] if nb == 1 else jnp.concatenate(pb, axis=1))
        for h, sl in enumerate(heads):
            pv = _dot(ps[h], v_ref[pl.ds(start, nb * t), sl])
            acc_sc[h] = pv if diagonal else acc_sc[h] + pv

    def loop(step):
        step(i, 1, True)
        pairs = lax.shift_right_logical(i, 1)

        def body(jj, carry):
            step(2 * jj, 2, False)
            return carry
        lax.fori_loop(0, pairs, body, 0)

        @pl.when(jnp.bitwise_and(i, 1) == 1)
        def _():
            step(i - 1, 1, False)

    loop(max_step)
    for h in range(DA_HEADS):
        m_sc[h] = jnp.broadcast_to(jnp.max(m_sc[h], axis=-1, keepdims=True), (2 * t, LANES))
    loop(sum_step)

    lv = lam_ref[...]
    lam = (jnp.exp(jnp.sum(lv[0:1] * lv[1:2], axis=-1, keepdims=True))
           - jnp.exp(jnp.sum(lv[2:3] * lv[3:4], axis=-1, keepdims=True)) + lam_init)
    for h, sl in enumerate(heads):
        num = acc_sc[h]
        den = jnp.sum(l_sc[h], axis=-1, keepdims=True)
        o = num[:t] / den[:t] - lam * (num[t:] / den[t:])
        o = o * lax.rsqrt(jnp.mean(o * o, axis=-1, keepdims=True) + NORM_EPS) * sub_ref[...]
        o = o * (1.0 - lam_init)
        z = z_ref[:, sl].astype(F32)
        o_ref[:, sl] = (o * _silu(z)).astype(BF16)


def _diff_attn(main, lam_vecs, subln, batch, seq, lam_init):
    t = ATT_TILE
    nq = seq // t
    return pl.pallas_call(
        functools.partial(_diff_attn_kernel, lam_init=lam_init),
        grid=(batch, nq),
        in_specs=[
            _resident((4, DA_QK_DIM)),
            pl.BlockSpec((t, WIDTH), lambda b, i: (b * nq + i, COL_DA_Q)),
            pl.BlockSpec((seq, WIDTH), lambda b, i: (b, COL_DA_K)),
            pl.BlockSpec((seq, WIDTH), lambda b, i: (b, COL_DA_V)),
            pl.BlockSpec((t, WIDTH), lambda b, i: (b * nq + i, COL_DA_Z)),
            _resident((1, DA_V_DIM)),
        ],
        out_specs=pl.BlockSpec((t, WIDTH), lambda b, i: (b * nq + i, 0)),
        out_shape=jax.ShapeDtypeStruct((batch * seq, WIDTH), BF16),
        scratch_shapes=[pltpu.VMEM((DA_HEADS, 2 * t, LANES), BF16),
                        pltpu.VMEM((DA_HEADS, 2 * t, LANES), F32),
                        pltpu.VMEM((DA_HEADS, 2 * t, LANES), F32),
                        pltpu.VMEM((DA_HEADS, 2 * t, DA_V_DIM), F32),
                        pltpu.VMEM((nq, DA_HEADS, 2 * t, t), F32)],
        compiler_params=_params("parallel", "arbitrary"),
        name="diff_attn",
    )(lam_vecs, main, main, main, main, subln)


def _rwkv_kernel(r_ref, k_ref, v_ref, z_ref, lo_ref, mur_ref, muk_ref, muv_ref, mulo_ref,
                 w0_ref, a0_ref, kk_ref, ka_ref, rk_ref, lng_ref, lnb_ref,
                 wup_ref, aup_ref, bd_ref, tri_ref,
                 o_ref, cr_sc, ck_sc, cv_sc, clo_sc, state_sc):
    @pl.when(pl.program_id(1) == 0)
    def _():
        cr_sc[...] = jnp.zeros(cr_sc.shape, F32)
        ck_sc[...] = jnp.zeros(ck_sc.shape, F32)
        cv_sc[...] = jnp.zeros(cv_sc.shape, F32)
        clo_sc[...] = jnp.zeros(clo_sc.shape, F32)
        state_sc[...] = jnp.zeros(state_sc.shape, F32)

    gts = [state_sc[p] for p in range(RW_HEADS // 2)]
    _software_pipeline([
        _rwkv_rows(sub, gts, r_ref, k_ref, v_ref, z_ref, lo_ref, mur_ref, muk_ref, muv_ref,
                   mulo_ref, w0_ref, a0_ref, kk_ref, ka_ref, rk_ref, lng_ref, lnb_ref, wup_ref,
                   aup_ref, bd_ref, tri_ref, o_ref, cr_sc, ck_sc, cv_sc, clo_sc)
        for sub in range(RW_SUBSTEPS)])
    for p in range(RW_HEADS // 2):
        state_sc[p] = gts[p]


def _software_pipeline(gens):
    nxt = [next(g) for g in gens]

    def emit(s):
        nxt[s] = next(gens[s], None)

    def drain(s, phase):
        while 0 <= s < len(gens) and nxt[s] == phase:
            emit(s)

    drain(0, "prep")
    for s in range(len(gens)):
        while nxt[s] == "main":
            emit(s)
            for _ in range(SIDE_PIECES_PER_MAIN):
                if s + 1 < len(gens) and nxt[s + 1] == "prep":
                    emit(s + 1)
                if s >= 1 and nxt[s - 1] == "tail":
                    emit(s - 1)
        drain(s + 1, "prep")
        drain(s - 1, "tail")
    drain(len(gens) - 1, "tail")


def _rwkv_rows(sub, gts, r_ref, k_ref, v_ref, z_ref, lo_ref, mur_ref, muk_ref, muv_ref, mulo_ref,
               w0_ref, a0_ref, kk_ref, ka_ref, rk_ref, lng_ref, lnb_ref,
               wup_ref, aup_ref, bd_ref, tri_ref,
               o_ref, cr_sc, ck_sc, cv_sc, clo_sc):
    n = CHUNK
    rows = RW_STEP_CHUNKS * n
    blk = pl.ds(sub * rows, rows)
    yield "prep"

    def shift(p, carry_ref, mu):
        first = lax.broadcasted_iota(jnp.int32, p.shape, 0) == 0
        prev = jnp.where(first, carry_ref[...], pltpu.roll(p, 1, 0))
        carry_ref[...] = p[rows - 1:rows, :]
        return p + (prev - p) * mu

    r = shift(r_ref[blk, :].astype(F32), cr_sc, mur_ref[...])
    yield "prep"
    k = shift(k_ref[blk, :].astype(F32), ck_sc, muk_ref[...])
    yield "prep"
    v = shift(v_ref[blk, :].astype(F32), cv_sc, muv_ref[...])
    lo = shift(lo_ref[blk, :], clo_sc, mulo_ref[...])
    yield "prep"

    def dot_hi(x, w3_ref):
        xh = x.astype(BF16)
        xl = (x - xh.astype(F32)).astype(BF16)
        return _dot(jnp.concatenate([xh, xl, xh], axis=1), w3_ref[...])

    wlog = w0_ref[...] + dot_hi(jnp.tanh(lo), wup_ref)
    alog = a0_ref[...] + dot_hi(lo, aup_ref)
    yield "prep"
    tneg = -wlog
    softplus = jnp.maximum(tneg, 0.0) + jnp.log(1.0 + jnp.exp(-jnp.abs(tneg)))
    logdecay = -jnp.exp(-softplus - 0.5)
    a = _sigmoid(alog)
    yield "prep"

    bd = bd_ref[...]
    kk = k * kk_ref[...]
    kk2 = kk * kk
    kk2h = kk2.astype(BF16)
    kk2l = (kk2 - kk2h.astype(F32)).astype(BF16)
    yield "prep"
    ssq = _dot(jnp.concatenate([kk2h, kk2l], axis=1), bd)
    kk = kk / jnp.maximum(jnp.sqrt(ssq), 1e-12)
    kmod = k * (1.0 + (a - 1.0) * ka_ref[...])
    yield "prep"

    yield "prep"
    cum = _dot(tri_ref[...], jnp.concatenate(_split3(logdecay), axis=0))
    yield "prep"
    e_in = jnp.exp(cum)
    e_ex = jnp.exp(cum - logdecay)
    e_inv = jnp.exp(-cum)
    yield "prep"
    a_t = -kk * e_ex
    r_t = r * e_in
    yield "prep"
    b_t = kk * a * e_inv
    k_t = kmod * e_inv
    rkb = r * kmod * rk_ref[...]
    z = z_ref[blk, :].astype(F32)
    lng, lnb = lng_ref[...], lnb_ref[...]
    yield "prep"

    lane = lax.broadcasted_iota(jnp.int32, (n, LANES), 1)
    first_head = lane < RW_HEAD

    def stack(xp):
        zero = jnp.zeros_like(xp)
        return jnp.concatenate([jnp.where(first_head, xp, zero),
                                jnp.where(first_head, zero, xp)], axis=0)

    def per_head(col):
        return jnp.where(first_head, col[:n], col[n:])

    ti = lax.broadcasted_iota(jnp.int32, (n, LANES), 0)
    si = jnp.where(first_head, lane, lane - RW_HEAD)
    strict = ti > si
    incl = ti >= si
    eye = (ti == si).astype(F32)
    same_head = ((lax.broadcasted_iota(jnp.int32, (LANES, LANES), 0) < RW_HEAD)
                 == (lax.broadcasted_iota(jnp.int32, (LANES, LANES), 1) < RW_HEAD))

    chains = [(c, p) for c in range(RW_STEP_CHUNKS) for p in range(RW_HEADS // 2)]

    def operands(c, p):
        rs = slice(c * n, (c + 1) * n)
        sl = slice(p * LANES, (p + 1) * LANES)
        pe = e_in[(c + 1) * n - 1:(c + 1) * n, sl]
        b_p, k_p, v_p = b_t[rs, sl], k_t[rs, sl], v[rs, sl]
        v_pb = v_p.astype(BF16)
        return dict(
            ar=jnp.concatenate([a_t[rs, sl], r_t[rs, sl]], axis=0).astype(BF16),
            a_st=stack(a_t[rs, sl].astype(BF16)), r_p=r_t[rs, sl], v_pb=v_pb, v_st=stack(v_pb),
            pe=pe,
            bk_st=jnp.concatenate([stack(b_p.astype(BF16)), stack(k_p.astype(BF16))], axis=0),
            bk_out=jnp.concatenate([b_p * pe, k_p * pe], axis=0).astype(BF16),
            bonus=per_head(jnp.sum(stack(rkb[rs, sl]), axis=-1, keepdims=True)) * v_p)

    ops = []
    for c, p in chains:
        ops.append(operands(c, p))
        yield "prep" if len(ops) < len(chains) else "main"
    for o in ops:
        m = _dot_nt(o["ar"], o["bk_st"])
        o["x"] = jnp.where(strict, m[:n, :LANES], 0.0)
        o["a_ak"] = jnp.where(strict, m[:n, LANES:], 0.0).astype(BF16)
        o["t_rb"] = jnp.where(incl, m[n:, :LANES], 0.0).astype(BF16)
        o["t_rk"] = jnp.where(incl, m[n:, LANES:], 0.0).astype(BF16)
        o["tinv"] = eye + o["x"]
    yield "main"
    levels = int(math.log2(n)) - 1
    for o in ops:
        xb = o["x"].astype(BF16)
        o["x"] = _dot(xb, stack(xb))
    yield "main"
    for level in range(levels):
        for o in ops:
            xb = o["x"].astype(BF16)
            if level + 1 < levels:
                both = _dot(jnp.concatenate([o["tinv"].astype(BF16), xb], axis=0), stack(xb))
                o["tinv"] = o["tinv"] + both[:n]
                o["x"] = both[n:]
            else:
                o["tinv"] = o["tinv"] + _dot(o["tinv"].astype(BF16), stack(xb))
        yield "main"
    for o in ops:
        o["tb"] = o["tinv"].astype(BF16)
        o["w"] = _dot(o["tb"], o["a_st"]).astype(BF16)
        o["av"] = _dot(o["a_ak"], o["v_st"]).astype(BF16)
    yield "main"
    for o in ops:
        o["u0"] = _dot(o["tb"], stack(o["av"])).astype(BF16)
        o["rw"] = (o["r_p"] + _dot(o["t_rb"], stack(o["w"]))).astype(BF16)
        wbk = _dot_tn(o["w"], o["bk_out"][:n])
        o["wbk"] = jnp.where(same_head, wbk, 0.0).astype(BF16)
    yield "main"
    for o in ops:
        o["y0"] = _dot(jnp.concatenate([o["t_rb"], o["t_rk"]], axis=1),
                       jnp.concatenate([stack(o["u0"]), o["v_st"]], axis=0))
        s0t = _dot_tn(jnp.concatenate([o["u0"], o["v_pb"]], axis=0), o["bk_out"])
        o["s0t"] = jnp.where(same_head, s0t, 0.0)

    pairs = range(RW_HEADS // 2)
    for c in range(RW_STEP_CHUNKS):
        yield "tail"
        for p in pairs:
            o = ops[c * len(pairs) + p]
            gtb = gts[p].astype(BF16)
            o["y"] = _dot_nt(o["rw"], gtb) + o["y0"]
            gts[p] = gts[p] * o["pe"] + _dot(gtb, o["wbk"]) + o["s0t"]
    for c in range(RW_STEP_CHUNKS):
        rs = slice(c * n, (c + 1) * n)
        for p in pairs:
            yield "tail"
            sl = slice(p * LANES, (p + 1) * LANES)
            y = ops[c * len(pairs) + p]["y"]
            mean = per_head(jnp.sum(stack(y), axis=-1, keepdims=True)) * (1.0 / RW_HEAD)
            d = y - mean
            var = per_head(jnp.sum(stack(d * d), axis=-1, keepdims=True)) * (1.0 / RW_HEAD)
            yn = d * lax.rsqrt(var + GN_EPS)
            out = yn * lng[:, sl] + lnb[:, sl] + ops[c * len(pairs) + p]["bonus"]
            o_ref[pl.ds(sub * rows + c * n, n), sl] = (out * _silu(z[rs, sl])).astype(BF16)


def _rwkv(main, lora, prm, batch, seq):
    n = RW_SUBSTEPS * RW_STEP_CHUNKS * CHUNK
    nc = seq // n
    rmap = lambda col: (lambda b, c: (b * nc + c, col))
    vec = lambda: _resident((1, WIDTH))
    return pl.pallas_call(
        _rwkv_kernel,
        grid=(batch, nc),
        in_specs=[
            pl.BlockSpec((n, WIDTH), rmap(COL_RW_R)),
            pl.BlockSpec((n, WIDTH), rmap(COL_RW_K)),
            pl.BlockSpec((n, WIDTH), rmap(COL_RW_V)),
            pl.BlockSpec((n, WIDTH), rmap(COL_RW_Z)),
            pl.BlockSpec((n, LANES), rmap(0)),
            vec(), vec(), vec(), _resident((1, LANES)),
            vec(), vec(), vec(), vec(), vec(), vec(), vec(),
            _resident((3 * LANES, WIDTH)), _resident((3 * LANES, WIDTH)),
            _resident((2 * WIDTH, WIDTH)),
            _resident((RW_STEP_CHUNKS * CHUNK, 3 * RW_STEP_CHUNKS * CHUNK)),
        ],
        out_specs=pl.BlockSpec((n, WIDTH), rmap(0)),
        out_shape=jax.ShapeDtypeStruct((batch * seq, WIDTH), BF16),
        scratch_shapes=[pltpu.VMEM((1, WIDTH), F32), pltpu.VMEM((1, WIDTH), F32),
                        pltpu.VMEM((1, WIDTH), F32), pltpu.VMEM((1, LANES), F32),
                        pltpu.VMEM((RW_HEADS // 2, LANES, LANES), F32)],
        compiler_params=_params("parallel", "arbitrary"),
        name="rwkv7",
    )(main, main, main, main, lora, *prm)


def _mem_kv_kernel(mem_ref, g_ref, w_ref, kg_ref, k_ref, v_ref):
    x = mem_ref[...]
    inv = lax.rsqrt(jnp.mean(x * x, axis=-1, keepdims=True) + NORM_EPS)
    h = (x * inv * g_ref[...]).astype(BF16)
    kv = _dot(h, w_ref[...])
    for hd in range(CA_HEADS):
        sl = slice(hd * CA_HEAD_DIM, (hd + 1) * CA_HEAD_DIM)
        kh = kv[:, sl]
        kh = kh * lax.rsqrt(jnp.mean(kh * kh, axis=-1, keepdims=True) + NORM_EPS) * kg_ref[...]
        k_ref[:, sl] = kh.astype(BF16)
    v_ref[...] = kv[:, WIDTH:].astype(BF16)


def _mem_kv(mem2, g, w, kg, batch, mem_len):
    row = lambda b: (b, 0)
    return pl.pallas_call(
        _mem_kv_kernel,
        grid=(batch,),
        in_specs=[pl.BlockSpec((mem_len, D_MODEL), row), _resident((1, D_MODEL)),
                  _resident((D_MODEL, 2 * WIDTH)), _resident((1, CA_HEAD_DIM))],
        out_specs=[pl.BlockSpec((mem_len, WIDTH), row), pl.BlockSpec((mem_len, WIDTH), row)],
        out_shape=[jax.ShapeDtypeStruct((batch * mem_len, WIDTH), BF16)] * 2,
        compiler_params=_params("parallel"),
        name="mem_kv",
    )(mem2, g, w, kg)


def _mem_attn_kernel(q_ref, z_ref, k_ref, v_ref, qg_ref, o_ref):
    gain = qg_ref[...] * (CA_HEAD_DIM ** -0.5 * LOG2_E)
    heads = tuple(slice(hd * CA_HEAD_DIM, (hd + 1) * CA_HEAD_DIM) for hd in range(CA_HEADS))
    qs = []
    for sl in heads:
        q = q_ref[:, sl].astype(F32)
        q = q * lax.rsqrt(jnp.mean(q * q, axis=-1, keepdims=True) + NORM_EPS) * gain
        qs.append(q.astype(BF16))
    ss = [_dot_nt(q, k_ref[:, sl]) for q, sl in zip(qs, heads)]
    ps = [jnp.exp2(s - jnp.max(s, axis=-1, keepdims=True)) for s in ss]
    ls = [jnp.sum(p, axis=-1, keepdims=True) for p in ps]
    os_ = [_dot(p.astype(BF16), v_ref[:, sl]) for p, sl in zip(ps, heads)]
    for o, l, sl in zip(os_, ls, heads):
        z = z_ref[:, sl].astype(F32)
        o_ref[:, sl] = (o / l * _silu(z)).astype(BF16)


def _mem_attn(main, km, vm, qg, batch, seq, mem_len):
    t = MEM_Q_TILE
    nq = seq // t
    return pl.pallas_call(
        _mem_attn_kernel,
        grid=(batch, nq),
        in_specs=[
            pl.BlockSpec((t, WIDTH), lambda b, i: (b * nq + i, COL_CA_Q)),
            pl.BlockSpec((t, WIDTH), lambda b, i: (b * nq + i, COL_CA_Z)),
            pl.BlockSpec((mem_len, WIDTH), lambda b, i: (b, 0)),
            pl.BlockSpec((mem_len, WIDTH), lambda b, i: (b, 0)),
            _resident((1, CA_HEAD_DIM)),
        ],
        out_specs=pl.BlockSpec((t, WIDTH), lambda b, i: (b * nq + i, 0)),
        out_shape=jax.ShapeDtypeStruct((batch * seq, WIDTH), BF16),
        compiler_params=_params("parallel", "parallel"),
        name="mem_attn",
    )(main, main, km, vm, qg)


def _merge_kernel(x_ref, ya_ref, yb_ref, yc_ref, ga_ref, gb_ref, gc_ref, wb_ref, wo_ref, o_ref):
    merged = None
    for n, (y_ref, g_ref) in enumerate(((ya_ref, ga_ref), (yb_ref, gb_ref), (yc_ref, gc_ref))):
        term = _sigmoid(g_ref[...].astype(F32)) * _dot(y_ref[...], wb_ref[n])
        merged = term if merged is None else merged + term
    o_ref[...] = x_ref[...] + _dot(merged.astype(BF16), wo_ref[...])


def _merge(x2, ya, yb, yc, main, wb, wo):
    tokens = x2.shape[0]
    tm = ROW_TILE
    row = lambda i: (i, 0)
    gate = lambda n: (lambda i: (i, COL_GATES // 2 + n))
    return pl.pallas_call(
        _merge_kernel,
        grid=(tokens // tm,),
        in_specs=[
            pl.BlockSpec((tm, D_MODEL), row),
            pl.BlockSpec((tm, WIDTH), row), pl.BlockSpec((tm, WIDTH), row),
            pl.BlockSpec((tm, WIDTH), row),
            pl.BlockSpec((tm, D_MODEL), gate(0)), pl.BlockSpec((tm, D_MODEL), gate(1)),
            pl.BlockSpec((tm, D_MODEL), gate(2)),
            _resident((3, WIDTH, D_MODEL)), _resident((D_MODEL, D_MODEL)),
        ],
        out_specs=pl.BlockSpec((tm, D_MODEL), row),
        out_shape=jax.ShapeDtypeStruct((tokens, D_MODEL), F32),
        compiler_params=_params("parallel"),
        name="merge",
    )(x2, ya, yb, yc, main, main, main, wb, wo)


def _rope_tables(seq):
    rot = 2 * ROPE_HALF
    inv = 1.0 / (ROPE_THETA ** (jnp.arange(0, rot, 2, dtype=F32) / rot))
    ang = jnp.arange(seq, dtype=F32)[:, None] * inv[None, :]
    cos, sin = jnp.cos(ang), jnp.sin(ang)
    pad = DA_QK_DIM - rot
    ones = jnp.ones((seq, pad), F32)
    zeros = jnp.zeros((seq, pad), F32)
    zh = jnp.zeros((seq, ROPE_HALF), F32)
    cosf = jnp.concatenate([cos, cos, ones], axis=1)
    sinm = jnp.concatenate([-sin, zh, zeros], axis=1)
    sinp = jnp.concatenate([zh, sin, zeros], axis=1)
    tile = lambda t: jnp.tile(t, (1, LANES // DA_QK_DIM))
    return tile(cosf), tile(sinm), tile(sinp)


def _block_diag_ones(width, block, dtype):
    i = jnp.arange(width) // block
    return (i[:, None] == i[None, :]).astype(dtype)


def _chunk_tril(size, chunk):
    i = jnp.arange(size)
    same = (i[:, None] // chunk) == (i[None, :] // chunk)
    return (same & (i[None, :] <= i[:, None])).astype(BF16)


def _reorder_w_in(w):
    cols = lambda lo, n: w[:, lo:lo + n]
    da = cols(0, 2048)
    rw_rkv = cols(2048, 1536)
    lora = cols(3584, 2 * RW_LORA)
    rw_z = cols(3712, 512)
    rest = cols(4224, 1024 + 3072)
    return jnp.concatenate([da, rw_rkv, rw_z, rest], axis=1).astype(BF16), lora.astype(BF16)


def _constants(seq):
    bd64 = _block_diag_ones(WIDTH, DA_QK_DIM, BF16)
    tri = _chunk_tril(RW_STEP_CHUNKS * CHUNK, CHUNK)
    return dict(rope=_rope_tables(seq), bd64=bd64, bd64x2=jnp.concatenate([bd64, bd64], axis=0),
                tri3=jnp.concatenate([tri, tri, tri], axis=1))


def _layer(x2, mem2, l, p, cst, batch, seq, mem_len):
    row = lambda t: t.reshape(1, -1).astype(F32)
    zpad = jnp.zeros((RW_LORA, WIDTH), F32)

    def hi_hi_lo(w):
        hi = w.astype(BF16)
        return jnp.concatenate([hi, hi, (w - hi.astype(F32)).astype(BF16)], axis=0)

    lam_init = 0.8 - 0.6 * math.exp(-0.3 * l)
    w_main, w_lora = _reorder_w_in(p["w_in"][l])
    qg = row(jnp.tile(p["da_q_norm"][l], WIDTH // DA_QK_DIM)) * (DA_QK_DIM ** -0.5 * LOG2_E)
    kg = row(jnp.tile(p["da_k_norm"][l], WIDTH // DA_QK_DIM))
    main, lora = _in_proj(x2, row(p["norm_g"][l]), w_main, w_lora, qg, kg, *cst["rope"],
                          cst["bd64"], seq)

    ya = _diff_attn(main, p["da_lambda"][l].astype(F32), row(p["da_subln"][l]), batch, seq,
                    lam_init)

    mu = p["rw_mu"][l]
    wup = hi_hi_lo(jnp.concatenate([p["rw_w_up"][l], zpad], axis=0))
    aup = hi_hi_lo(jnp.concatenate([zpad, p["rw_a_up"][l]], axis=0))
    prm = (row(mu[:512]), row(mu[512:1024]), row(mu[1024:1536]), row(mu[1536:]),
           row(p["rw_w0"][l]), row(p["rw_a0"][l]), row(p["rw_k_k"][l]), row(p["rw_k_a"][l]),
           row(p["rw_r_k"][l]), row(p["rw_ln_g"][l]), row(p["rw_ln_b"][l]), wup, aup,
           cst["bd64x2"], cst["tri3"])
    yb = _rwkv(main, lora, prm, batch, seq)

    km, vm = _mem_kv(mem2, row(p["mem_norm_g"][l]), p["w_mem_kv"][l].astype(BF16),
                     row(p["ca_k_norm"][l]), batch, mem_len)
    yc = _mem_attn(main, km, vm, row(p["ca_q_norm"][l]), batch, seq, mem_len)

    x2 = _merge(x2, ya, yb, yc, main, p["w_branch"][l].astype(BF16), p["w_out"][l].astype(BF16))
    return x2, ya, yb, yc


def kernel(x, mem, norm_g, mem_norm_g, w_in, w_mem_kv, da_q_norm, da_k_norm, da_lambda, da_subln,
           rw_mu, rw_w0, rw_w_up, rw_a0, rw_a_up, rw_k_k, rw_k_a, rw_r_k, rw_ln_g, rw_ln_b,
           ca_q_norm, ca_k_norm, w_branch, w_out):
    batch, seq, _ = x.shape
    mem_len = mem.shape[1]
    assert seq % ROW_TILE == 0 and seq % ATT_TILE == 0 and seq % MEM_Q_TILE == 0
    assert seq % (RW_SUBSTEPS * RW_STEP_CHUNKS * CHUNK) == 0 and mem_len % 8 == 0
    p = dict(norm_g=norm_g, mem_norm_g=mem_norm_g, w_in=w_in, w_mem_kv=w_mem_kv,
             da_q_norm=da_q_norm, da_k_norm=da_k_norm, da_lambda=da_lambda, da_subln=da_subln,
             rw_mu=rw_mu, rw_w0=rw_w0, rw_w_up=rw_w_up, rw_a0=rw_a0, rw_a_up=rw_a_up,
             rw_k_k=rw_k_k, rw_k_a=rw_k_a, rw_r_k=rw_r_k, rw_ln_g=rw_ln_g, rw_ln_b=rw_ln_b,
             ca_q_norm=ca_q_norm, ca_k_norm=ca_k_norm, w_branch=w_branch, w_out=w_out)
    cst = _constants(seq)
    x2 = x.reshape(batch * seq, D_MODEL)
    mem2 = mem.reshape(batch * mem_len, D_MODEL)
    for l in range(w_in.shape[0]):
        x2 = _layer(x2, mem2, l, p, cst, batch, seq, mem_len)[0]
    return x2.reshape(batch, seq, D_MODEL)
```

```python
import functools
import math

import jax
import jax.numpy as jnp
from jax import lax
from jax.experimental import pallas as pl
from jax.experimental.pallas import tpu as pltpu

F32 = jnp.float32
BF16 = jnp.bfloat16

D_MODEL = 1024
WIDTH = 512
DA_HEADS = 4
DA_QK_DIM = 64
DA_V_DIM = 128
RW_HEADS = 8
RW_HEAD = 64
RW_LORA = 64
CA_HEADS = 4
CA_HEAD_DIM = 128
ROPE_THETA = 500000.0
ROPE_HALF = DA_QK_DIM // 8
NORM_EPS = 1e-6
LOG2_E = 1.4426950408889634
GN_EPS = 64e-5

LANES = 128
MAIN_WIDTH = 8192
VMEM_LIMIT = 56 * 1024 * 1024

COL_DA_Q, COL_DA_K, COL_DA_V, COL_DA_Z = 0, 1, 2, 3
COL_RW_R, COL_RW_K, COL_RW_V, COL_RW_Z = 4, 5, 6, 7
COL_CA_Q, COL_CA_Z = 8, 9
COL_GATES = 10

ROW_TILE = 512
ATT_TILE = 256
CHUNK = 64
RW_STEP_CHUNKS = 4
RW_SUBSTEPS = 2
SIDE_PIECES_PER_MAIN = 2


def _params(*sem):
    return pltpu.CompilerParams(dimension_semantics=sem, vmem_limit_bytes=VMEM_LIMIT)


def _resident(shape):
    nd = len(shape)
    return pl.BlockSpec(shape, lambda *_: (0,) * nd, pipeline_mode=pl.Buffered(1))


def _silu(z):
    return z * (1.0 / (1.0 + jnp.exp(-z)))


def _sigmoid(z):
    return 1.0 / (1.0 + jnp.exp(-z))


def _split3(x):
    hi = x.astype(BF16)
    r1 = x - hi.astype(F32)
    mid = r1.astype(BF16)
    lo = (r1 - mid.astype(F32)).astype(BF16)
    return hi, mid, lo


def _dot(a, b):
    return jnp.dot(a, b, preferred_element_type=F32)


def _dot_nt(a, b):
    return lax.dot_general(a, b, (((1,), (1,)), ((), ())), preferred_element_type=F32)


def _dot_tn(a, b):
    return lax.dot_general(a, b, (((0,), (0,)), ((), ())), preferred_element_type=F32)


def _in_proj_kernel(x_ref, g_ref, w_ref, wlo_ref, qg_ref, kg_ref, cos_ref, sinm_ref, sinp_ref,
                    bd_ref, km_ref, vm_ref, cqg_ref, main_ref, lora_ref, yc_ref):
    x = x_ref[...]
    inv = lax.rsqrt(jnp.mean(x * x, axis=-1, keepdims=True) + NORM_EPS)
    h = (x * inv * g_ref[...]).astype(BF16)
    cosf, sinm, sinp = cos_ref[...], sinm_ref[...], sinp_ref[...]

    ca = {}
    heads = tuple(slice(hd * CA_HEAD_DIM, (hd + 1) * CA_HEAD_DIM) for hd in range(CA_HEADS))

    def ca_queries():
        gain = cqg_ref[...] * (CA_HEAD_DIM ** -0.5 * LOG2_E)
        ca["q"] = []
        for sl in heads:
            q = ca["q_raw"][:, sl]
            q = q * lax.rsqrt(jnp.mean(q * q, axis=-1, keepdims=True) + NORM_EPS) * gain
            ca["q"].append(q.astype(BF16))

    def ca_scores():
        ca["s"] = [_dot_nt(q, km_ref[:, sl]) for q, sl in zip(ca["q"], heads)]

    def ca_probs():
        ca["p"] = [jnp.exp2(s - jnp.max(s, axis=-1, keepdims=True)) for s in ca["s"]]
        ca["l"] = [jnp.sum(p, axis=-1, keepdims=True) for p in ca["p"]]

    def ca_values():
        ca["o"] = [_dot(p.astype(BF16), vm_ref[:, sl]) for p, sl in zip(ca["p"], heads)]

    def ca_store():
        for o, l, sl in zip(ca["o"], ca["l"], heads):
            yc_ref[:, sl] = (o / l * _silu(ca["z"][:, sl])).astype(BF16)

    ca_stages = iter((ca_queries, ca_scores, ca_probs, ca_values, ca_store))
    order = (COL_CA_Q, COL_CA_Z) + tuple(c for c in range(MAIN_WIDTH // WIDTH)
                                         if c not in (COL_CA_Q, COL_CA_Z))
    assert len(order) >= 2 * 5

    for n, c in enumerate(order):
        acc = _dot(h, w_ref[:, c * WIDTH:(c + 1) * WIDTH])
        if c == COL_CA_Q:
            ca["q_raw"] = acc
        if c == COL_CA_Z:
            ca["z"] = acc
        if c in (COL_DA_Q, COL_DA_K):
            gain = qg_ref[...] if c == COL_DA_Q else kg_ref[...]
            ssq = _dot((acc * acc).astype(BF16), bd_ref[...])
            acc = acc * lax.rsqrt(ssq * (1.0 / DA_QK_DIM) + NORM_EPS) * gain
            for s in range(WIDTH // LANES):
                t = acc[:, s * LANES:(s + 1) * LANES]
                t = (t * cosf + pltpu.roll(t, LANES - ROPE_HALF, 1) * sinm
                     + pltpu.roll(t, ROPE_HALF, 1) * sinp)
                lo = c * WIDTH + s * LANES
                main_ref[:, lo:lo + LANES] = t.astype(BF16)
        else:
            main_ref[:, c * WIDTH:(c + 1) * WIDTH] = acc.astype(BF16)
        if n % 2 == 1:
            next(ca_stages, lambda: None)()
    lora_ref[...] = _dot(h, wlo_ref[...])


def _in_proj(x2, g, w_main, w_lora, qg, kg, cosf, sinm, sinp, bd, km, vm, cqg, seq, mem_len):
    tokens = x2.shape[0]
    tm = ROW_TILE
    pos_blocks = seq // tm
    row = lambda i: (i, 0)
    pos = lambda i: (i % pos_blocks, 0)
    mem = lambda i: (i // pos_blocks, 0)
    return pl.pallas_call(
        _in_proj_kernel,
        grid=(tokens // tm,),
        in_specs=[
            pl.BlockSpec((tm, D_MODEL), row),
            _resident((1, D_MODEL)),
            _resident((D_MODEL, MAIN_WIDTH)),
            _resident((D_MODEL, LANES)),
            _resident((1, WIDTH)),
            _resident((1, WIDTH)),
            pl.BlockSpec((tm, LANES), pos),
            pl.BlockSpec((tm, LANES), pos),
            pl.BlockSpec((tm, LANES), pos),
            _resident((WIDTH, WIDTH)),
            pl.BlockSpec((mem_len, WIDTH), mem),
            pl.BlockSpec((mem_len, WIDTH), mem),
            _resident((1, CA_HEAD_DIM)),
        ],
        out_specs=[pl.BlockSpec((tm, MAIN_WIDTH), row), pl.BlockSpec((tm, LANES), row),
                   pl.BlockSpec((tm, WIDTH), row)],
        out_shape=[jax.ShapeDtypeStruct((tokens, MAIN_WIDTH), BF16),
                   jax.ShapeDtypeStruct((tokens, LANES), F32),
                   jax.ShapeDtypeStruct((tokens, WIDTH), BF16)],
        compiler_params=_params("parallel"),
        name="in_proj",
    )(x2, g, w_main, w_lora, qg, kg, cosf, sinm, sinp, bd, km, vm, cqg)


def _diff_attn_kernel(lam_ref, q_ref, k_ref, v_ref, z_ref, sub_ref, o_ref,
                      qs_sc, m_sc, l_sc, acc_sc, s_sc, *, lam_init):
    t = ATT_TILE
    i = pl.program_id(1)
    lane = lax.broadcasted_iota(jnp.int32, (t, LANES), 1)
    heads = tuple(slice(h * LANES, (h + 1) * LANES) for h in range(DA_HEADS))
    for h, sl in enumerate(heads):
        q = q_ref[:, sl]
        zero = jnp.zeros_like(q)
        qs_sc[h] = jnp.concatenate([jnp.where(lane < DA_QK_DIM, q, zero),
                                    jnp.where(lane >= DA_QK_DIM, q, zero)], axis=0)
    def max_step(j0, nb, diagonal):
        ss = {}
        for b in range(nb):
            start = pl.multiple_of((j0 + b) * t, t)
            for h, sl in enumerate(heads):
                s = _dot_nt(qs_sc[h], k_ref[pl.ds(start, t), sl])
                if diagonal:
                    r = lax.broadcasted_iota(jnp.int32, (2 * t, t), 0)
                    r = jnp.where(r >= t, r - t, r)
                    c = lax.broadcasted_iota(jnp.int32, (2 * t, t), 1)
                    s = jnp.where(r >= c, s, -jnp.inf)
                ss[b, h] = s
        for h in range(DA_HEADS):
            fold = None
            for b in range(nb):
                s = ss[b, h]
                s_sc[j0 + b, h] = s
                fb = jnp.maximum(s[:, :LANES], s[:, LANES:])
                fold = fb if fold is None else jnp.maximum(fold, fb)
            m_sc[h] = fold if diagonal else jnp.maximum(m_sc[h], fold)

    def sum_step(j0, nb, diagonal):
        start = pl.multiple_of(j0 * t, t)
        ps = []
        for h in range(DA_HEADS):
            m = m_sc[h]
            mm = jnp.concatenate([m, m], axis=1)
            part, pb = None, []
            for b in range(nb):
                p = jnp.exp2(s_sc[j0 + b, h] - mm)
                pp = p[:, :LANES] + p[:, LANES:]
                part = pp if part is None else part + pp
                pb.append(p.astype(BF16))
            l_sc[h] = part if diagonal else l_sc[h] + part
            ps.append(pb[0] if nb == 1 else jnp.concatenate(pb, axis=1))
        for h, sl in enumerate(heads):
            pv = _dot(ps[h], v_ref[pl.ds(start, nb * t), sl])
            acc_sc[h] = pv if diagonal else acc_sc[h] + pv

    def loop(step):
        step(i, 1, True)
        pairs = lax.shift_right_logical(i, 1)

        def body(jj, carry):
            step(2 * jj, 2, False)
            return carry
        lax.fori_loop(0, pairs, body, 0)

        @pl.when(jnp.bitwise_and(i, 1) == 1)
        def _():
            step(i - 1, 1, False)

    loop(max_step)
    for h in range(DA_HEADS):
        m_sc[h] = jnp.broadcast_to(jnp.max(m_sc[h], axis=-1, keepdims=True), (2 * t, LANES))
    loop(sum_step)

    lv = lam_ref[...]
    lam = (jnp.exp(jnp.sum(lv[0:1] * lv[1:2], axis=-1, keepdims=True))
           - jnp.exp(jnp.sum(lv[2:3] * lv[3:4], axis=-1, keepdims=True)) + lam_init)
    for h, sl in enumerate(heads):
        num = acc_sc[h]
        den = jnp.sum(l_sc[h], axis=-1, keepdims=True)
        o = num[:t] / den[:t] - lam * (num[t:] / den[t:])
        o = o * lax.rsqrt(jnp.mean(o * o, axis=-1, keepdims=True) + NORM_EPS) * sub_ref[...]
        o = o * (1.0 - lam_init)
        z = z_ref[:, sl].astype(F32)
        o_ref[:, sl] = (o * _silu(z)).astype(BF16)


def _diff_attn(main, lam_vecs, subln, batch, seq, lam_init):
    t = ATT_TILE
    nq = seq // t
    return pl.pallas_call(
        functools.partial(_diff_attn_kernel, lam_init=lam_init),
        grid=(batch, nq),
        in_specs=[
            _resident((4, DA_QK_DIM)),
            pl.BlockSpec((t, WIDTH), lambda b, i: (b * nq + i, COL_DA_Q)),
            pl.BlockSpec((seq, WIDTH), lambda b, i: (b, COL_DA_K)),
            pl.BlockSpec((seq, WIDTH), lambda b, i: (b, COL_DA_V)),
            pl.BlockSpec((t, WIDTH), lambda b, i: (b * nq + i, COL_DA_Z)),
            _resident((1, DA_V_DIM)),
        ],
        out_specs=pl.BlockSpec((t, WIDTH), lambda b, i: (b * nq + i, 0)),
        out_shape=jax.ShapeDtypeStruct((batch * seq, WIDTH), BF16),
        scratch_shapes=[pltpu.VMEM((DA_HEADS, 2 * t, LANES), BF16),
                        pltpu.VMEM((DA_HEADS, 2 * t, LANES), F32),
                        pltpu.VMEM((DA_HEADS, 2 * t, LANES), F32),
                        pltpu.VMEM((DA_HEADS, 2 * t, DA_V_DIM), F32),
                        pltpu.VMEM((nq, DA_HEADS, 2 * t, t), F32)],
        compiler_params=_params("parallel", "arbitrary"),
        name="diff_attn",
    )(lam_vecs, main, main, main, main, subln)


def _rwkv_kernel(r_ref, k_ref, v_ref, z_ref, lo_ref, mur_ref, muk_ref, muv_ref, mulo_ref,
                 w0_ref, a0_ref, kk_ref, ka_ref, rk_ref, lng_ref, lnb_ref,
                 wup_ref, aup_ref, bd_ref, tri_ref,
                 o_ref, cr_sc, ck_sc, cv_sc, clo_sc, state_sc):
    @pl.when(pl.program_id(1) == 0)
    def _():
        cr_sc[...] = jnp.zeros(cr_sc.shape, F32)
        ck_sc[...] = jnp.zeros(ck_sc.shape, F32)
        cv_sc[...] = jnp.zeros(cv_sc.shape, F32)
        clo_sc[...] = jnp.zeros(clo_sc.shape, F32)
        state_sc[...] = jnp.zeros(state_sc.shape, F32)

    gts = [state_sc[p] for p in range(RW_HEADS // 2)]
    _software_pipeline([
        _rwkv_rows(sub, gts, r_ref, k_ref, v_ref, z_ref, lo_ref, mur_ref, muk_ref, muv_ref,
                   mulo_ref, w0_ref, a0_ref, kk_ref, ka_ref, rk_ref, lng_ref, lnb_ref, wup_ref,
                   aup_ref, bd_ref, tri_ref, o_ref, cr_sc, ck_sc, cv_sc, clo_sc)
        for sub in range(RW_SUBSTEPS)])
    for p in range(RW_HEADS // 2):
        state_sc[p] = gts[p]


def _software_pipeline(gens):
    nxt = [next(g) for g in gens]

    def emit(s):
        nxt[s] = next(gens[s], None)

    def drain(s, phase):
        while 0 <= s < len(gens) and nxt[s] == phase:
            emit(s)

    drain(0, "prep")
    for s in range(len(gens)):
        while nxt[s] == "main":
            emit(s)
            for _ in range(SIDE_PIECES_PER_MAIN):
                if s + 1 < len(gens) and nxt[s + 1] == "prep":
                    emit(s + 1)
                if s >= 1 and nxt[s - 1] == "tail":
                    emit(s - 1)
        drain(s + 1, "prep")
        drain(s - 1, "tail")
    drain(len(gens) - 1, "tail")


def _rwkv_rows(sub, gts, r_ref, k_ref, v_ref, z_ref, lo_ref, mur_ref, muk_ref, muv_ref, mulo_ref,
               w0_ref, a0_ref, kk_ref, ka_ref, rk_ref, lng_ref, lnb_ref,
               wup_ref, aup_ref, bd_ref, tri_ref,
               o_ref, cr_sc, ck_sc, cv_sc, clo_sc):
    n = CHUNK
    rows = RW_STEP_CHUNKS * n
    blk = pl.ds(sub * rows, rows)
    yield "prep"

    def shift(p, carry_ref, mu):
        first = lax.broadcasted_iota(jnp.int32, p.shape, 0) == 0
        prev = jnp.where(first, carry_ref[...], pltpu.roll(p, 1, 0))
        carry_ref[...] = p[rows - 1:rows, :]
        return p + (prev - p) * mu

    r = shift(r_ref[blk, :].astype(F32), cr_sc, mur_ref[...])
    yield "prep"
    k = shift(k_ref[blk, :].astype(F32), ck_sc, muk_ref[...])
    yield "prep"
    v = shift(v_ref[blk, :].astype(F32), cv_sc, muv_ref[...])
    lo = shift(lo_ref[blk, :], clo_sc, mulo_ref[...])
    yield "prep"

    def dot_hi(x, w3_ref):
        xh = x.astype(BF16)
        xl = (x - xh.astype(F32)).astype(BF16)
        return _dot(jnp.concatenate([xh, xl, xh], axis=1), w3_ref[...])

    wlog = w0_ref[...] + dot_hi(jnp.tanh(lo), wup_ref)
    alog = a0_ref[...] + dot_hi(lo, aup_ref)
    yield "prep"
    tneg = -wlog
    softplus = jnp.maximum(tneg, 0.0) + jnp.log(1.0 + jnp.exp(-jnp.abs(tneg)))
    logdecay = -jnp.exp(-softplus - 0.5)
    a = _sigmoid(alog)
    yield "prep"

    bd = bd_ref[...]
    kk = k * kk_ref[...]
    kk2 = kk * kk
    kk2h = kk2.astype(BF16)
    kk2l = (kk2 - kk2h.astype(F32)).astype(BF16)
    yield "prep"
    ssq = _dot(jnp.concatenate([kk2h, kk2l], axis=1), bd)
    kk = kk / jnp.maximum(jnp.sqrt(ssq), 1e-12)
    kmod = k * (1.0 + (a - 1.0) * ka_ref[...])
    yield "prep"

    yield "prep"
    cum = _dot(tri_ref[...], jnp.concatenate(_split3(logdecay), axis=0))
    yield "prep"
    e_in = jnp.exp(cum)
    e_ex = jnp.exp(cum - logdecay)
    e_inv = jnp.exp(-cum)
    yield "prep"
    a_t = -kk * e_ex
    r_t = r * e_in
    yield "prep"
    b_t = kk * a * e_inv
    k_t = kmod * e_inv
    rkb = r * kmod * rk_ref[...]
    z = z_ref[blk, :].astype(F32)
    lng, lnb = lng_ref[...], lnb_ref[...]
    yield "prep"

    lane = lax.broadcasted_iota(jnp.int32, (n, LANES), 1)
    first_head = lane < RW_HEAD

    def stack(xp):
        zero = jnp.zeros_like(xp)
        return jnp.concatenate([jnp.where(first_head, xp, zero),
                                jnp.where(first_head, zero, xp)], axis=0)

    def per_head(col):
        return jnp.where(first_head, col[:n], col[n:])

    ti = lax.broadcasted_iota(jnp.int32, (n, LANES), 0)
    si = jnp.where(first_head, lane, lane - RW_HEAD)
    strict = ti > si
    incl = ti >= si
    eye = (ti == si).astype(F32)
    same_head = ((lax.broadcasted_iota(jnp.int32, (LANES, LANES), 0) < RW_HEAD)
                 == (lax.broadcasted_iota(jnp.int32, (LANES, LANES), 1) < RW_HEAD))

    chains = [(c, p) for c in range(RW_STEP_CHUNKS) for p in range(RW_HEADS // 2)]

    def operands(c, p):
        rs = slice(c * n, (c + 1) * n)
        sl = slice(p * LANES, (p + 1) * LANES)
        pe = e_in[(c + 1) * n - 1:(c + 1) * n, sl]
        b_p, k_p, v_p = b_t[rs, sl], k_t[rs, sl], v[rs, sl]
        v_pb = v_p.astype(BF16)
        return dict(
            ar=jnp.concatenate([a_t[rs, sl], r_t[rs, sl]], axis=0).astype(BF16),
            a_st=stack(a_t[rs, sl].astype(BF16)), r_p=r_t[rs, sl], v_pb=v_pb, v_st=stack(v_pb),
            pe=pe,
            bk_st=jnp.concatenate([stack(b_p.astype(BF16)), stack(k_p.astype(BF16))], axis=0),
            bk_out=jnp.concatenate([b_p * pe, k_p * pe], axis=0).astype(BF16),
            bonus=per_head(jnp.sum(stack(rkb[rs, sl]), axis=-1, keepdims=True)) * v_p)

    ops = []
    for c, p in chains:
        ops.append(operands(c, p))
        yield "prep" if len(ops) < len(chains) else "main"
    for o in ops:
        m = _dot_nt(o["ar"], o["bk_st"])
        o["x"] = jnp.where(strict, m[:n, :LANES], 0.0)
        o["a_ak"] = jnp.where(strict, m[:n, LANES:], 0.0).astype(BF16)
        o["t_rb"] = jnp.where(incl, m[n:, :LANES], 0.0).astype(BF16)
        o["t_rk"] = jnp.where(incl, m[n:, LANES:], 0.0).astype(BF16)
        o["tinv"] = eye + o["x"]
    yield "main"
    levels = int(math.log2(n)) - 1
    for o in ops:
        xb = o["x"].astype(BF16)
        o["x"] = _dot(xb, stack(xb))
    yield "main"
    for level in range(levels):
        for o in ops:
            xb = o["x"].astype(BF16)
            if level + 1 < levels:
                both = _dot(jnp.concatenate([o["tinv"].astype(BF16), xb], axis=0), stack(xb))
                o["tinv"] = o["tinv"] + both[:n]
                o["x"] = both[n:]
            else:
                o["tinv"] = o["tinv"] + _dot(o["tinv"].astype(BF16), stack(xb))
        yield "main"
    for o in ops:
        o["tb"] = o["tinv"].astype(BF16)
        o["w"] = _dot(o["tb"], o["a_st"]).astype(BF16)
        o["av"] = _dot(o["a_ak"], o["v_st"]).astype(BF16)
    yield "main"
    for o in ops:
        o["u0"] = _dot(o["tb"], stack(o["av"])).astype(BF16)
        o["rw"] = (o["r_p"] + _dot(o["t_rb"], stack(o["w"]))).astype(BF16)
        wbk = _dot_tn(o["w"], o["bk_out"][:n])
        o["wbk"] = jnp.where(same_head, wbk, 0.0).astype(BF16)
    yield "main"
    for o in ops:
        o["y0"] = _dot(jnp.concatenate([o["t_rb"], o["t_rk"]], axis=1),
                       jnp.concatenate([stack(o["u0"]), o["v_st"]], axis=0))
        s0t = _dot_tn(jnp.concatenate([o["u0"], o["v_pb"]], axis=0), o["bk_out"])
        o["s0t"] = jnp.where(same_head, s0t, 0.0)

    pairs = range(RW_HEADS // 2)
    for c in range(RW_STEP_CHUNKS):
        yield "tail"
        for p in pairs:
            o = ops[c * len(pairs) + p]
            gtb = gts[p].astype(BF16)
            o["y"] = _dot_nt(o["rw"], gtb) + o["y0"]
            gts[p] = gts[p] * o["pe"] + _dot(gtb, o["wbk"]) + o["s0t"]
    for c in range(RW_STEP_CHUNKS):
        rs = slice(c * n, (c + 1) * n)
        for p in pairs:
            yield "tail"
            sl = slice(p * LANES, (p + 1) * LANES)
            y = ops[c * len(pairs) + p]["y"]
            mean = per_head(jnp.sum(stack(y), axis=-1, keepdims=True)) * (1.0 / RW_HEAD)
            d = y - mean
            var = per_head(jnp.sum(stack(d * d), axis=-1, keepdims=True)) * (1.0 / RW_HEAD)
            yn = d * lax.rsqrt(var + GN_EPS)
            out = yn * lng[:, sl] + lnb[:, sl] + ops[c * len(pairs) + p]["bonus"]
            o_ref[pl.ds(sub * rows + c * n, n), sl] = (out * _silu(z[rs, sl])).astype(BF16)


def _rwkv(main, lora, prm, batch, seq):
    n = RW_SUBSTEPS * RW_STEP_CHUNKS * CHUNK
    nc = seq // n
    rmap = lambda col: (lambda b, c: (b * nc + c, col))
    vec = lambda: _resident((1, WIDTH))
    return pl.pallas_call(
        _rwkv_kernel,
        grid=(batch, nc),
        in_specs=[
            pl.BlockSpec((n, WIDTH), rmap(COL_RW_R)),
            pl.BlockSpec((n, WIDTH), rmap(COL_RW_K)),
            pl.BlockSpec((n, WIDTH), rmap(COL_RW_V)),
            pl.BlockSpec((n, WIDTH), rmap(COL_RW_Z)),
            pl.BlockSpec((n, LANES), rmap(0)),
            vec(), vec(), vec(), _resident((1, LANES)),
            vec(), vec(), vec(), vec(), vec(), vec(), vec(),
            _resident((3 * LANES, WIDTH)), _resident((3 * LANES, WIDTH)),
            _resident((2 * WIDTH, WIDTH)),
            _resident((RW_STEP_CHUNKS * CHUNK, 3 * RW_STEP_CHUNKS * CHUNK)),
        ],
        out_specs=pl.BlockSpec((n, WIDTH), rmap(0)),
        out_shape=jax.ShapeDtypeStruct((batch * seq, WIDTH), BF16),
        scratch_shapes=[pltpu.VMEM((1, WIDTH), F32), pltpu.VMEM((1, WIDTH), F32),
                        pltpu.VMEM((1, WIDTH), F32), pltpu.VMEM((1, LANES), F32),
                        pltpu.VMEM((RW_HEADS // 2, LANES, LANES), F32)],
        compiler_params=_params("parallel", "arbitrary"),
        name="rwkv7",
    )(main, main, main, main, lora, *prm)


def _mem_kv_kernel(mem_ref, g_ref, w_ref, kg_ref, k_ref, v_ref):
    x = mem_ref[...]
    inv = lax.rsqrt(jnp.mean(x * x, axis=-1, keepdims=True) + NORM_EPS)
    h = (x * inv * g_ref[...]).astype(BF16)
    kv = _dot(h, w_ref[...])
    for hd in range(CA_HEADS):
        sl = slice(hd * CA_HEAD_DIM, (hd + 1) * CA_HEAD_DIM)
        kh = kv[:, sl]
        kh = kh * lax.rsqrt(jnp.mean(kh * kh, axis=-1, keepdims=True) + NORM_EPS) * kg_ref[...]
        k_ref[:, sl] = kh.astype(BF16)
    v_ref[...] = kv[:, WIDTH:].astype(BF16)


def _mem_kv(mem2, g, w, kg, batch, mem_len):
    row = lambda b: (b, 0)
    return pl.pallas_call(
        _mem_kv_kernel,
        grid=(batch,),
        in_specs=[pl.BlockSpec((mem_len, D_MODEL), row), _resident((1, D_MODEL)),
                  _resident((D_MODEL, 2 * WIDTH)), _resident((1, CA_HEAD_DIM))],
        out_specs=[pl.BlockSpec((mem_len, WIDTH), row), pl.BlockSpec((mem_len, WIDTH), row)],
        out_shape=[jax.ShapeDtypeStruct((batch * mem_len, WIDTH), BF16)] * 2,
        compiler_params=_params("parallel"),
        name="mem_kv",
    )(mem2, g, w, kg)


def _merge_kernel(x_ref, ya_ref, yb_ref, yc_ref, ga_ref, gb_ref, gc_ref, wb_ref, wo_ref, o_ref):
    merged = None
    for n, (y_ref, g_ref) in enumerate(((ya_ref, ga_ref), (yb_ref, gb_ref), (yc_ref, gc_ref))):
        term = _sigmoid(g_ref[...].astype(F32)) * _dot(y_ref[...], wb_ref[n])
        merged = term if merged is None else merged + term
    o_ref[...] = x_ref[...] + _dot(merged.astype(BF16), wo_ref[...])


def _merge(x2, ya, yb, yc, main, wb, wo):
    tokens = x2.shape[0]
    tm = ROW_TILE
    row = lambda i: (i, 0)
    gate = lambda n: (lambda i: (i, COL_GATES // 2 + n))
    return pl.pallas_call(
        _merge_kernel,
        grid=(tokens // tm,),
        in_specs=[
            pl.BlockSpec((tm, D_MODEL), row),
            pl.BlockSpec((tm, WIDTH), row), pl.BlockSpec((tm, WIDTH), row),
            pl.BlockSpec((tm, WIDTH), row),
            pl.BlockSpec((tm, D_MODEL), gate(0)), pl.BlockSpec((tm, D_MODEL), gate(1)),
            pl.BlockSpec((tm, D_MODEL), gate(2)),
            _resident((3, WIDTH, D_MODEL)), _resident((D_MODEL, D_MODEL)),
        ],
        out_specs=pl.BlockSpec((tm, D_MODEL), row),
        out_shape=jax.ShapeDtypeStruct((tokens, D_MODEL), F32),
        compiler_params=_params("parallel"),
        name="merge",
    )(x2, ya, yb, yc, main, main, main, wb, wo)


def _rope_tables(seq):
    rot = 2 * ROPE_HALF
    inv = 1.0 / (ROPE_THETA ** (jnp.arange(0, rot, 2, dtype=F32) / rot))
    ang = jnp.arange(seq, dtype=F32)[:, None] * inv[None, :]
    cos, sin = jnp.cos(ang), jnp.sin(ang)
    pad = DA_QK_DIM - rot
    ones = jnp.ones((seq, pad), F32)
    zeros = jnp.zeros((seq, pad), F32)
    zh = jnp.zeros((seq, ROPE_HALF), F32)
    cosf = jnp.concatenate([cos, cos, ones], axis=1)
    sinm = jnp.concatenate([-sin, zh, zeros], axis=1)
    sinp = jnp.concatenate([zh, sin, zeros], axis=1)
    tile = lambda t: jnp.tile(t, (1, LANES // DA_QK_DIM))
    return tile(cosf), tile(sinm), tile(sinp)


def _block_diag_ones(width, block, dtype):
    i = jnp.arange(width) // block
    return (i[:, None] == i[None, :]).astype(dtype)


def _chunk_tril(size, chunk):
    i = jnp.arange(size)
    same = (i[:, None] // chunk) == (i[None, :] // chunk)
    return (same & (i[None, :] <= i[:, None])).astype(BF16)


def _reorder_w_in(w):
    cols = lambda lo, n: w[:, lo:lo + n]
    da = cols(0, 2048)
    rw_rkv = cols(2048, 1536)
    lora = cols(3584, 2 * RW_LORA)
    rw_z = cols(3712, 512)
    rest = cols(4224, 1024 + 3072)
    return jnp.concatenate([da, rw_rkv, rw_z, rest], axis=1).astype(BF16), lora.astype(BF16)


def _constants(seq):
    bd64 = _block_diag_ones(WIDTH, DA_QK_DIM, BF16)
    tri = _chunk_tril(RW_STEP_CHUNKS * CHUNK, CHUNK)
    return dict(rope=_rope_tables(seq), bd64=bd64, bd64x2=jnp.concatenate([bd64, bd64], axis=0),
                tri3=jnp.concatenate([tri, tri, tri], axis=1))


def _layer(x2, mem2, l, p, cst, batch, seq, mem_len):
    row = lambda t: t.reshape(1, -1).astype(F32)
    zpad = jnp.zeros((RW_LORA, WIDTH), F32)

    def hi_hi_lo(w):
        hi = w.astype(BF16)
        return jnp.concatenate([hi, hi, (w - hi.astype(F32)).astype(BF16)], axis=0)

    lam_init = 0.8 - 0.6 * math.exp(-0.3 * l)
    w_main, w_lora = _reorder_w_in(p["w_in"][l])
    qg = row(jnp.tile(p["da_q_norm"][l], WIDTH // DA_QK_DIM)) * (DA_QK_DIM ** -0.5 * LOG2_E)
    kg = row(jnp.tile(p["da_k_norm"][l], WIDTH // DA_QK_DIM))
    km, vm = _mem_kv(mem2, row(p["mem_norm_g"][l]), p["w_mem_kv"][l].astype(BF16),
                     row(p["ca_k_norm"][l]), batch, mem_len)
    main, lora, yc = _in_proj(x2, row(p["norm_g"][l]), w_main, w_lora, qg, kg, *cst["rope"],
                              cst["bd64"], km, vm, row(p["ca_q_norm"][l]), seq, mem_len)

    ya = _diff_attn(main, p["da_lambda"][l].astype(F32), row(p["da_subln"][l]), batch, seq,
                    lam_init)

    mu = p["rw_mu"][l]
    wup = hi_hi_lo(jnp.concatenate([p["rw_w_up"][l], zpad], axis=0))
    aup = hi_hi_lo(jnp.concatenate([zpad, p["rw_a_up"][l]], axis=0))
    prm = (row(mu[:512]), row(mu[512:1024]), row(mu[1024:1536]), row(mu[1536:]),
           row(p["rw_w0"][l]), row(p["rw_a0"][l]), row(p["rw_k_k"][l]), row(p["rw_k_a"][l]),
           row(p["rw_r_k"][l]), row(p["rw_ln_g"][l]), row(p["rw_ln_b"][l]), wup, aup,
           cst["bd64x2"], cst["tri3"])
    yb = _rwkv(main, lora, prm, batch, seq)

    x2 = _merge(x2, ya, yb, yc, main, p["w_branch"][l].astype(BF16), p["w_out"][l].astype(BF16))
    return x2, ya, yb, yc


def kernel(x, mem, norm_g, mem_norm_g, w_in, w_mem_kv, da_q_norm, da_k_norm, da_lambda, da_subln,
           rw_mu, rw_w0, rw_w_up, rw_a0, rw_a_up, rw_k_k, rw_k_a, rw_r_k, rw_ln_g, rw_ln_b,
           ca_q_norm, ca_k_norm, w_branch, w_out):
    batch, seq, _ = x.shape
    mem_len = mem.shape[1]
    assert seq % ROW_TILE == 0 and seq % ATT_TILE == 0
    assert seq % (RW_SUBSTEPS * RW_STEP_CHUNKS * CHUNK) == 0 and mem_len % 8 == 0
    p = dict(norm_g=norm_g, mem_norm_g=mem_norm_g, w_in=w_in, w_mem_kv=w_mem_kv,
             da_q_norm=da_q_norm, da_k_norm=da_k_norm, da_lambda=da_lambda, da_subln=da_subln,
             rw_mu=rw_mu, rw_w0=rw_w0, rw_w_up=rw_w_up, rw_a0=rw_a0, rw_a_up=rw_a_up,
             rw_k_k=rw_k_k, rw_k_a=rw_k_a, rw_r_k=rw_r_k, rw_ln_g=rw_ln_g, rw_ln_b=rw_ln_b,
             ca_q_norm=ca_q_norm, ca_k_norm=ca_k_norm, w_branch=w_branch, w_out=w_out)
    cst = _constants(seq)
    x2 = x.reshape(batch * seq, D_MODEL)
    mem2 = mem.reshape(batch * mem_len, D_MODEL)
    for l in range(w_in.shape[0]):
        x2 = _layer(x2, mem2, l, p, cst, batch, seq, mem_len)[0]
    return x2.reshape(batch, seq, D_MODEL)
```

```python
import functools
import math

import jax
import jax.numpy as jnp
from jax import lax
from jax.experimental import pallas as pl
from jax.experimental.pallas import tpu as pltpu

F32 = jnp.float32
BF16 = jnp.bfloat16

D_MODEL = 1024
WIDTH = 512
DA_HEADS = 4
DA_QK_DIM = 64
DA_V_DIM = 128
RW_HEADS = 8
RW_HEAD = 64
RW_LORA = 64
CA_HEADS = 4
CA_HEAD_DIM = 128
ROPE_THETA = 500000.0
ROPE_HALF = DA_QK_DIM // 8
NORM_EPS = 1e-6
LOG2_E = 1.4426950408889634
GN_EPS = 64e-5

LANES = 128
MAIN_WIDTH = 8192
VMEM_LIMIT = 56 * 1024 * 1024

COL_DA_Q, COL_DA_K, COL_DA_V, COL_DA_Z = 0, 1, 2, 3
COL_RW_R, COL_RW_K, COL_RW_V, COL_RW_Z = 4, 5, 6, 7
COL_CA_Q, COL_CA_Z = 8, 9
COL_GATES = 10

ROW_TILE = 512
ATT_TILE = 256
CHUNK = 64
RW_STEP_CHUNKS = 4
RW_SUBSTEPS = 2
SIDE_PIECES_PER_MAIN = 2


def _params(*sem):
    return pltpu.CompilerParams(dimension_semantics=sem, vmem_limit_bytes=VMEM_LIMIT)


def _resident(shape):
    nd = len(shape)
    return pl.BlockSpec(shape, lambda *_: (0,) * nd, pipeline_mode=pl.Buffered(1))


def _silu(z):
    return z * (1.0 / (1.0 + jnp.exp(-z)))


def _sigmoid(z):
    return 1.0 / (1.0 + jnp.exp(-z))


def _split3(x):
    hi = x.astype(BF16)
    r1 = x - hi.astype(F32)
    mid = r1.astype(BF16)
    lo = (r1 - mid.astype(F32)).astype(BF16)
    return hi, mid, lo


def _dot(a, b):
    return jnp.dot(a, b, preferred_element_type=F32)


def _dot_nt(a, b):
    return lax.dot_general(a, b, (((1,), (1,)), ((), ())), preferred_element_type=F32)


def _dot_tn(a, b):
    return lax.dot_general(a, b, (((0,), (0,)), ((), ())), preferred_element_type=F32)


def _in_proj_kernel(x_ref, g_ref, w_ref, wlo_ref, qg_ref, kg_ref, cos_ref, sinm_ref, sinp_ref,
                    bd_ref, km_ref, vm_ref, cqg_ref, main_ref, lora_ref, yc_ref):
    x = x_ref[...]
    inv = lax.rsqrt(jnp.mean(x * x, axis=-1, keepdims=True) + NORM_EPS)
    h = (x * inv * g_ref[...]).astype(BF16)
    cosf, sinm, sinp = cos_ref[...], sinm_ref[...], sinp_ref[...]

    ca = {}
    heads = tuple(slice(hd * CA_HEAD_DIM, (hd + 1) * CA_HEAD_DIM) for hd in range(CA_HEADS))

    def ca_queries():
        gain = cqg_ref[...] * (CA_HEAD_DIM ** -0.5 * LOG2_E)
        ca["q"] = []
        for sl in heads:
            q = ca["q_raw"][:, sl]
            q = q * lax.rsqrt(jnp.mean(q * q, axis=-1, keepdims=True) + NORM_EPS) * gain
            ca["q"].append(q.astype(BF16))

    def ca_scores():
        ca["s"] = [_dot_nt(q, km_ref[:, sl]) for q, sl in zip(ca["q"], heads)]

    def ca_probs():
        ca["p"] = [jnp.exp2(s - jnp.max(s, axis=-1, keepdims=True)) for s in ca["s"]]
        ca["l"] = [jnp.sum(p, axis=-1, keepdims=True) for p in ca["p"]]

    def ca_values():
        ca["o"] = [_dot(p.astype(BF16), vm_ref[:, sl]) for p, sl in zip(ca["p"], heads)]

    def ca_store():
        for o, l, sl in zip(ca["o"], ca["l"], heads):
            yc_ref[:, sl] = (o / l * _silu(ca["z"][:, sl])).astype(BF16)

    def head_norm_rope(c, acc, gain_ref):
        st = {}

        def squares():
            st["sq"] = (acc * acc).astype(BF16)

        def sums():
            st["ssq"] = _dot(st["sq"], bd_ref[...])

        def finish():
            y = acc * lax.rsqrt(st["ssq"] * (1.0 / DA_QK_DIM) + NORM_EPS) * gain_ref[...]
            for s in range(WIDTH // LANES):
                t = y[:, s * LANES:(s + 1) * LANES]
                t = (t * cosf + pltpu.roll(t, LANES - ROPE_HALF, 1) * sinm
                     + pltpu.roll(t, ROPE_HALF, 1) * sinp)
                lo = c * WIDTH + s * LANES
                main_ref[:, lo:lo + LANES] = t.astype(BF16)

        return [squares, sums, finish]

    queue = []
    order = (COL_CA_Q, COL_CA_Z, COL_DA_Q, COL_DA_K) + tuple(
        c for c in range(MAIN_WIDTH // WIDTH)
        if c not in (COL_CA_Q, COL_CA_Z, COL_DA_Q, COL_DA_K))
    for c in order:
        acc = _dot(h, w_ref[:, c * WIDTH:(c + 1) * WIDTH])
        if c not in (COL_DA_Q, COL_DA_K):
            main_ref[:, c * WIDTH:(c + 1) * WIDTH] = acc.astype(BF16)
        if c == COL_CA_Q:
            ca["q_raw"] = acc
            queue += [ca_queries, ca_scores, ca_probs, ca_values]
        elif c == COL_CA_Z:
            ca["z"] = acc
            queue.append(ca_store)
        elif c == COL_DA_Q:
            queue += head_norm_rope(c, acc, qg_ref)
        elif c == COL_DA_K:
            queue += head_norm_rope(c, acc, kg_ref)
        if queue:
            queue.pop(0)()
    lora_ref[...] = _dot(h, wlo_ref[...])
    for stage in queue:
        stage()


def _in_proj(x2, g, w_main, w_lora, qg, kg, cosf, sinm, sinp, bd, km, vm, cqg, seq, mem_len):
    tokens = x2.shape[0]
    tm = ROW_TILE
    pos_blocks = seq // tm
    row = lambda i: (i, 0)
    pos = lambda i: (i % pos_blocks, 0)
    mem = lambda i: (i // pos_blocks, 0)
    return pl.pallas_call(
        _in_proj_kernel,
        grid=(tokens // tm,),
        in_specs=[
            pl.BlockSpec((tm, D_MODEL), row),
            _resident((1, D_MODEL)),
            _resident((D_MODEL, MAIN_WIDTH)),
            _resident((D_MODEL, LANES)),
            _resident((1, WIDTH)),
            _resident((1, WIDTH)),
            pl.BlockSpec((tm, LANES), pos),
            pl.BlockSpec((tm, LANES), pos),
            pl.BlockSpec((tm, LANES), pos),
            _resident((WIDTH, WIDTH)),
            pl.BlockSpec((mem_len, WIDTH), mem),
            pl.BlockSpec((mem_len, WIDTH), mem),
            _resident((1, CA_HEAD_DIM)),
        ],
        out_specs=[pl.BlockSpec((tm, MAIN_WIDTH), row), pl.BlockSpec((tm, LANES), row),
                   pl.BlockSpec((tm, WIDTH), row)],
        out_shape=[jax.ShapeDtypeStruct((tokens, MAIN_WIDTH), BF16),
                   jax.ShapeDtypeStruct((tokens, LANES), F32),
                   jax.ShapeDtypeStruct((tokens, WIDTH), BF16)],
        compiler_params=_params("parallel"),
        name="in_proj",
    )(x2, g, w_main, w_lora, qg, kg, cosf, sinm, sinp, bd, km, vm, cqg)


def _diff_attn_kernel(lam_ref, q_ref, k_ref, v_ref, z_ref, sub_ref, o_ref,
                      qs_sc, m_sc, l_sc, acc_sc, s_sc, *, lam_init):
    t = ATT_TILE
    i = pl.program_id(1)
    lane = lax.broadcasted_iota(jnp.int32, (t, LANES), 1)
    heads = tuple(slice(h * LANES, (h + 1) * LANES) for h in range(DA_HEADS))
    for h, sl in enumerate(heads):
        q = q_ref[:, sl]
        zero = jnp.zeros_like(q)
        qs_sc[h] = jnp.concatenate([jnp.where(lane < DA_QK_DIM, q, zero),
                                    jnp.where(lane >= DA_QK_DIM, q, zero)], axis=0)
    def max_step(j0, nb, diagonal):
        ss = {}
        for b in range(nb):
            start = pl.multiple_of((j0 + b) * t, t)
            for h, sl in enumerate(heads):
                s = _dot_nt(qs_sc[h], k_ref[pl.ds(start, t), sl])
                if diagonal:
                    r = lax.broadcasted_iota(jnp.int32, (2 * t, t), 0)
                    r = jnp.where(r >= t, r - t, r)
                    c = lax.broadcasted_iota(jnp.int32, (2 * t, t), 1)
                    s = jnp.where(r >= c, s, -jnp.inf)
                ss[b, h] = s
        for h in range(DA_HEADS):
            fold = None
            for b in range(nb):
                s = ss[b, h]
                s_sc[j0 + b, h] = s
                fb = jnp.maximum(s[:, :LANES], s[:, LANES:])
                fold = fb if fold is None else jnp.maximum(fold, fb)
            m_sc[h] = fold if diagonal else jnp.maximum(m_sc[h], fold)

    def sum_step(j0, nb, diagonal):
        start = pl.multiple_of(j0 * t, t)
        ps = []
        for h in range(DA_HEADS):
            m = m_sc[h]
            mm = jnp.concatenate([m, m], axis=1)
            part, pb = None, []
            for b in range(nb):
                p = jnp.exp2(s_sc[j0 + b, h] - mm)
                pp = p[:, :LANES] + p[:, LANES:]
                part = pp if part is None else part + pp
                pb.append(p.astype(BF16))
            l_sc[h] = part if diagonal else l_sc[h] + part
            ps.append(pb[0] if nb == 1 else jnp.concatenate(pb, axis=1))
        for h, sl in enumerate(heads):
            pv = _dot(ps[h], v_ref[pl.ds(start, nb * t), sl])
            acc_sc[h] = pv if diagonal else acc_sc[h] + pv

    def loop(step):
        step(i, 1, True)
        pairs = lax.shift_right_logical(i, 1)

        def body(jj, carry):
            step(2 * jj, 2, False)
            return carry
        lax.fori_loop(0, pairs, body, 0)

        @pl.when(jnp.bitwise_and(i, 1) == 1)
        def _():
            step(i - 1, 1, False)

    loop(max_step)
    for h in range(DA_HEADS):
        m_sc[h] = jnp.broadcast_to(jnp.max(m_sc[h], axis=-1, keepdims=True), (2 * t, LANES))
    loop(sum_step)

    lv = lam_ref[...]
    lam = (jnp.exp(jnp.sum(lv[0:1] * lv[1:2], axis=-1, keepdims=True))
           - jnp.exp(jnp.sum(lv[2:3] * lv[3:4], axis=-1, keepdims=True)) + lam_init)
    for h, sl in enumerate(heads):
        num = acc_sc[h]
        den = jnp.sum(l_sc[h], axis=-1, keepdims=True)
        o = num[:t] / den[:t] - lam * (num[t:] / den[t:])
        o = o * lax.rsqrt(jnp.mean(o * o, axis=-1, keepdims=True) + NORM_EPS) * sub_ref[...]
        o = o * (1.0 - lam_init)
        z = z_ref[:, sl].astype(F32)
        o_ref[:, sl] = (o * _silu(z)).astype(BF16)


def _diff_attn(main, lam_vecs, subln, batch, seq, lam_init):
    t = ATT_TILE
    nq = seq // t
    return pl.pallas_call(
        functools.partial(_diff_attn_kernel, lam_init=lam_init),
        grid=(batch, nq),
        in_specs=[
            _resident((4, DA_QK_DIM)),
            pl.BlockSpec((t, WIDTH), lambda b, i: (b * nq + i, COL_DA_Q)),
            pl.BlockSpec((seq, WIDTH), lambda b, i: (b, COL_DA_K)),
            pl.BlockSpec((seq, WIDTH), lambda b, i: (b, COL_DA_V)),
            pl.BlockSpec((t, WIDTH), lambda b, i: (b * nq + i, COL_DA_Z)),
            _resident((1, DA_V_DIM)),
        ],
        out_specs=pl.BlockSpec((t, WIDTH), lambda b, i: (b * nq + i, 0)),
        out_shape=jax.ShapeDtypeStruct((batch * seq, WIDTH), BF16),
        scratch_shapes=[pltpu.VMEM((DA_HEADS, 2 * t, LANES), BF16),
                        pltpu.VMEM((DA_HEADS, 2 * t, LANES), F32),
                        pltpu.VMEM((DA_HEADS, 2 * t, LANES), F32),
                        pltpu.VMEM((DA_HEADS, 2 * t, DA_V_DIM), F32),
                        pltpu.VMEM((nq, DA_HEADS, 2 * t, t), F32)],
        compiler_params=_params("parallel", "arbitrary"),
        name="diff_attn",
    )(lam_vecs, main, main, main, main, subln)


def _rwkv_kernel(r_ref, k_ref, v_ref, z_ref, lo_ref, mur_ref, muk_ref, muv_ref, mulo_ref,
                 w0_ref, a0_ref, kk_ref, ka_ref, rk_ref, lng_ref, lnb_ref,
                 wup_ref, aup_ref, bd_ref, tri_ref,
                 o_ref, cr_sc, ck_sc, cv_sc, clo_sc, state_sc):
    @pl.when(pl.program_id(1) == 0)
    def _():
        cr_sc[...] = jnp.zeros(cr_sc.shape, F32)
        ck_sc[...] = jnp.zeros(ck_sc.shape, F32)
        cv_sc[...] = jnp.zeros(cv_sc.shape, F32)
        clo_sc[...] = jnp.zeros(clo_sc.shape, F32)
        state_sc[...] = jnp.zeros(state_sc.shape, F32)

    gts = [state_sc[p] for p in range(RW_HEADS // 2)]
    _software_pipeline([
        _rwkv_rows(sub, gts, r_ref, k_ref, v_ref, z_ref, lo_ref, mur_ref, muk_ref, muv_ref,
                   mulo_ref, w0_ref, a0_ref, kk_ref, ka_ref, rk_ref, lng_ref, lnb_ref, wup_ref,
                   aup_ref, bd_ref, tri_ref, o_ref, cr_sc, ck_sc, cv_sc, clo_sc)
        for sub in range(RW_SUBSTEPS)])
    for p in range(RW_HEADS // 2):
        state_sc[p] = gts[p]


def _software_pipeline(gens):
    nxt = [next(g) for g in gens]

    def emit(s):
        nxt[s] = next(gens[s], None)

    def drain(s, phase):
        while 0 <= s < len(gens) and nxt[s] == phase:
            emit(s)

    drain(0, "prep")
    for s in range(len(gens)):
        while nxt[s] == "main":
            emit(s)
            for _ in range(SIDE_PIECES_PER_MAIN):
                if s + 1 < len(gens) and nxt[s + 1] == "prep":
                    emit(s + 1)
                if s >= 1 and nxt[s - 1] == "tail":
                    emit(s - 1)
        drain(s + 1, "prep")
        drain(s - 1, "tail")
    drain(len(gens) - 1, "tail")


def _rwkv_rows(sub, gts, r_ref, k_ref, v_ref, z_ref, lo_ref, mur_ref, muk_ref, muv_ref, mulo_ref,
               w0_ref, a0_ref, kk_ref, ka_ref, rk_ref, lng_ref, lnb_ref,
               wup_ref, aup_ref, bd_ref, tri_ref,
               o_ref, cr_sc, ck_sc, cv_sc, clo_sc):
    n = CHUNK
    rows = RW_STEP_CHUNKS * n
    blk = pl.ds(sub * rows, rows)
    yield "prep"

    def shift(p, carry_ref, mu):
        first = lax.broadcasted_iota(jnp.int32, p.shape, 0) == 0
        prev = jnp.where(first, carry_ref[...], pltpu.roll(p, 1, 0))
        carry_ref[...] = p[rows - 1:rows, :]
        return p + (prev - p) * mu

    r = shift(r_ref[blk, :].astype(F32), cr_sc, mur_ref[...])
    yield "prep"
    k = shift(k_ref[blk, :].astype(F32), ck_sc, muk_ref[...])
    yield "prep"
    v = shift(v_ref[blk, :].astype(F32), cv_sc, muv_ref[...])
    lo = shift(lo_ref[blk, :], clo_sc, mulo_ref[...])
    yield "prep"

    def dot_hi(x, w3_ref):
        xh = x.astype(BF16)
        xl = (x - xh.astype(F32)).astype(BF16)
        return _dot(jnp.concatenate([xh, xl, xh], axis=1), w3_ref[...])

    wlog = w0_ref[...] + dot_hi(jnp.tanh(lo), wup_ref)
    alog = a0_ref[...] + dot_hi(lo, aup_ref)
    yield "prep"
    tneg = -wlog
    softplus = jnp.maximum(tneg, 0.0) + jnp.log(1.0 + jnp.exp(-jnp.abs(tneg)))
    logdecay = -jnp.exp(-softplus - 0.5)
    a = _sigmoid(alog)
    yield "prep"

    bd = bd_ref[...]
    kk = k * kk_ref[...]
    kk2 = kk * kk
    kk2h = kk2.astype(BF16)
    kk2l = (kk2 - kk2h.astype(F32)).astype(BF16)
    yield "prep"
    ssq = _dot(jnp.concatenate([kk2h, kk2l], axis=1), bd)
    kk = kk / jnp.maximum(jnp.sqrt(ssq), 1e-12)
    kmod = k * (1.0 + (a - 1.0) * ka_ref[...])
    yield "prep"

    yield "prep"
    cum = _dot(tri_ref[...], jnp.concatenate(_split3(logdecay), axis=0))
    yield "prep"
    e_in = jnp.exp(cum)
    e_ex = jnp.exp(cum - logdecay)
    e_inv = jnp.exp(-cum)
    yield "prep"
    a_t = -kk * e_ex
    r_t = r * e_in
    yield "prep"
    b_t = kk * a * e_inv
    k_t = kmod * e_inv
    rkb = r * kmod * rk_ref[...]
    z = z_ref[blk, :].astype(F32)
    lng, lnb = lng_ref[...], lnb_ref[...]
    yield "prep"

    lane = lax.broadcasted_iota(jnp.int32, (n, LANES), 1)
    first_head = lane < RW_HEAD

    def stack(xp):
        zero = jnp.zeros_like(xp)
        return jnp.concatenate([jnp.where(first_head, xp, zero),
                                jnp.where(first_head, zero, xp)], axis=0)

    def per_head(col):
        return jnp.where(first_head, col[:n], col[n:])

    ti = lax.broadcasted_iota(jnp.int32, (n, LANES), 0)
    si = jnp.where(first_head, lane, lane - RW_HEAD)
    strict = ti > si
    incl = ti >= si
    eye = (ti == si).astype(F32)
    same_head = ((lax.broadcasted_iota(jnp.int32, (LANES, LANES), 0) < RW_HEAD)
                 == (lax.broadcasted_iota(jnp.int32, (LANES, LANES), 1) < RW_HEAD))

    chains = [(c, p) for c in range(RW_STEP_CHUNKS) for p in range(RW_HEADS // 2)]

    def operands(c, p):
        rs = slice(c * n, (c + 1) * n)
        sl = slice(p * LANES, (p + 1) * LANES)
        pe = e_in[(c + 1) * n - 1:(c + 1) * n, sl]
        b_p, k_p, v_p = b_t[rs, sl], k_t[rs, sl], v[rs, sl]
        v_pb = v_p.astype(BF16)
        return dict(
            ar=jnp.concatenate([a_t[rs, sl], r_t[rs, sl]], axis=0).astype(BF16),
            a_st=stack(a_t[rs, sl].astype(BF16)), r_p=r_t[rs, sl], v_pb=v_pb, v_st=stack(v_pb),
            pe=pe,
            bk_st=jnp.concatenate([stack(b_p.astype(BF16)), stack(k_p.astype(BF16))], axis=0),
            bk_out=jnp.concatenate([b_p * pe, k_p * pe], axis=0).astype(BF16),
            bonus=per_head(jnp.sum(stack(rkb[rs, sl]), axis=-1, keepdims=True)) * v_p)

    ops = []
    for c, p in chains:
        ops.append(operands(c, p))
        yield "prep" if len(ops) < len(chains) else "main"
    for o in ops:
        m = _dot_nt(o["ar"], o["bk_st"])
        o["x"] = jnp.where(strict, m[:n, :LANES], 0.0)
        o["a_ak"] = jnp.where(strict, m[:n, LANES:], 0.0).astype(BF16)
        o["t_rb"] = jnp.where(incl, m[n:, :LANES], 0.0).astype(BF16)
        o["t_rk"] = jnp.where(incl, m[n:, LANES:], 0.0).astype(BF16)
        o["tinv"] = eye + o["x"]
    yield "main"
    levels = int(math.log2(n)) - 1
    for o in ops:
        xb = o["x"].astype(BF16)
        o["x"] = _dot(xb, stack(xb))
    yield "main"
    for level in range(levels):
        for o in ops:
            xb = o["x"].astype(BF16)
            if level + 1 < levels:
                both = _dot(jnp.concatenate([o["tinv"].astype(BF16), xb], axis=0), stack(xb))
                o["tinv"] = o["tinv"] + both[:n]
                o["x"] = both[n:]
            else:
                o["tinv"] = o["tinv"] + _dot(o["tinv"].astype(BF16), stack(xb))
        yield "main"
    for o in ops:
        o["tb"] = o["tinv"].astype(BF16)
        o["w"] = _dot(o["tb"], o["a_st"]).astype(BF16)
        o["av"] = _dot(o["a_ak"], o["v_st"]).astype(BF16)
    yield "main"
    for o in ops:
        o["u0"] = _dot(o["tb"], stack(o["av"])).astype(BF16)
        o["rw"] = (o["r_p"] + _dot(o["t_rb"], stack(o["w"]))).astype(BF16)
        wbk = _dot_tn(o["w"], o["bk_out"][:n])
        o["wbk"] = jnp.where(same_head, wbk, 0.0).astype(BF16)
    yield "main"
    for o in ops:
        o["y0"] = _dot(jnp.concatenate([o["t_rb"], o["t_rk"]], axis=1),
                       jnp.concatenate([stack(o["u0"]), o["v_st"]], axis=0))
        s0t = _dot_tn(jnp.concatenate([o["u0"], o["v_pb"]], axis=0), o["bk_out"])
        o["s0t"] = jnp.where(same_head, s0t, 0.0)

    pairs = range(RW_HEADS // 2)
    for c in range(RW_STEP_CHUNKS):
        yield "tail"
        for p in pairs:
            o = ops[c * len(pairs) + p]
            gtb = gts[p].astype(BF16)
            o["y"] = _dot_nt(o["rw"], gtb) + o["y0"]
            gts[p] = gts[p] * o["pe"] + _dot(gtb, o["wbk"]) + o["s0t"]
    for c in range(RW_STEP_CHUNKS):
        rs = slice(c * n, (c + 1) * n)
        for p in pairs:
            yield "tail"
            sl = slice(p * LANES, (p + 1) * LANES)
            y = ops[c * len(pairs) + p]["y"]
            mean = per_head(jnp.sum(stack(y), axis=-1, keepdims=True)) * (1.0 / RW_HEAD)
            d = y - mean
            var = per_head(jnp.sum(stack(d * d), axis=-1, keepdims=True)) * (1.0 / RW_HEAD)
            yn = d * lax.rsqrt(var + GN_EPS)
            out = yn * lng[:, sl] + lnb[:, sl] + ops[c * len(pairs) + p]["bonus"]
            o_ref[pl.ds(sub * rows + c * n, n), sl] = (out * _silu(z[rs, sl])).astype(BF16)


def _rwkv(main, lora, prm, batch, seq):
    n = RW_SUBSTEPS * RW_STEP_CHUNKS * CHUNK
    nc = seq // n
    rmap = lambda col: (lambda b, c: (b * nc + c, col))
    vec = lambda: _resident((1, WIDTH))
    return pl.pallas_call(
        _rwkv_kernel,
        grid=(batch, nc),
        in_specs=[
            pl.BlockSpec((n, WIDTH), rmap(COL_RW_R)),
            pl.BlockSpec((n, WIDTH), rmap(COL_RW_K)),
            pl.BlockSpec((n, WIDTH), rmap(COL_RW_V)),
            pl.BlockSpec((n, WIDTH), rmap(COL_RW_Z)),
            pl.BlockSpec((n, LANES), rmap(0)),
            vec(), vec(), vec(), _resident((1, LANES)),
            vec(), vec(), vec(), vec(), vec(), vec(), vec(),
            _resident((3 * LANES, WIDTH)), _resident((3 * LANES, WIDTH)),
            _resident((2 * WIDTH, WIDTH)),
            _resident((RW_STEP_CHUNKS * CHUNK, 3 * RW_STEP_CHUNKS * CHUNK)),
        ],
        out_specs=pl.BlockSpec((n, WIDTH), rmap(0)),
        out_shape=jax.ShapeDtypeStruct((batch * seq, WIDTH), BF16),
        scratch_shapes=[pltpu.VMEM((1, WIDTH), F32), pltpu.VMEM((1, WIDTH), F32),
                        pltpu.VMEM((1, WIDTH), F32), pltpu.VMEM((1, LANES), F32),
                        pltpu.VMEM((RW_HEADS // 2, LANES, LANES), F32)],
        compiler_params=_params("parallel", "arbitrary"),
        name="rwkv7",
    )(main, main, main, main, lora, *prm)


def _mem_kv_kernel(mem_ref, g_ref, w_ref, kg_ref, k_ref, v_ref):
    x = mem_ref[...]
    inv = lax.rsqrt(jnp.mean(x * x, axis=-1, keepdims=True) + NORM_EPS)
    h = (x * inv * g_ref[...]).astype(BF16)
    kv = _dot(h, w_ref[...])
    for hd in range(CA_HEADS):
        sl = slice(hd * CA_HEAD_DIM, (hd + 1) * CA_HEAD_DIM)
        kh = kv[:, sl]
        kh = kh * lax.rsqrt(jnp.mean(kh * kh, axis=-1, keepdims=True) + NORM_EPS) * kg_ref[...]
        k_ref[:, sl] = kh.astype(BF16)
    v_ref[...] = kv[:, WIDTH:].astype(BF16)


def _mem_kv(mem2, g, w, kg, batch, mem_len):
    row = lambda b: (b, 0)
    return pl.pallas_call(
        _mem_kv_kernel,
        grid=(batch,),
        in_specs=[pl.BlockSpec((mem_len, D_MODEL), row), _resident((1, D_MODEL)),
                  _resident((D_MODEL, 2 * WIDTH)), _resident((1, CA_HEAD_DIM))],
        out_specs=[pl.BlockSpec((mem_len, WIDTH), row), pl.BlockSpec((mem_len, WIDTH), row)],
        out_shape=[jax.ShapeDtypeStruct((batch * mem_len, WIDTH), BF16)] * 2,
        compiler_params=_params("parallel"),
        name="mem_kv",
    )(mem2, g, w, kg)


def _merge_kernel(x_ref, ya_ref, yb_ref, yc_ref, ga_ref, gb_ref, gc_ref, wb_ref, wo_ref, o_ref):
    merged = None
    for n, (y_ref, g_ref) in enumerate(((ya_ref, ga_ref), (yb_ref, gb_ref), (yc_ref, gc_ref))):
        term = _sigmoid(g_ref[...].astype(F32)) * _dot(y_ref[...], wb_ref[n])
        merged = term if merged is None else merged + term
    o_ref[...] = x_ref[...] + _dot(merged.astype(BF16), wo_ref[...])


def _merge(x2, ya, yb, yc, main, wb, wo):
    tokens = x2.shape[0]
    tm = ROW_TILE
    row = lambda i: (i, 0)
    gate = lambda n: (lambda i: (i, COL_GATES // 2 + n))
    return pl.pallas_call(
        _merge_kernel,
        grid=(tokens // tm,),
        in_specs=[
            pl.BlockSpec((tm, D_MODEL), row),
            pl.BlockSpec((tm, WIDTH), row), pl.BlockSpec((tm, WIDTH), row),
            pl.BlockSpec((tm, WIDTH), row),
            pl.BlockSpec((tm, D_MODEL), gate(0)), pl.BlockSpec((tm, D_MODEL), gate(1)),
            pl.BlockSpec((tm, D_MODEL), gate(2)),
            _resident((3, WIDTH, D_MODEL)), _resident((D_MODEL, D_MODEL)),
        ],
        out_specs=pl.BlockSpec((tm, D_MODEL), row),
        out_shape=jax.ShapeDtypeStruct((tokens, D_MODEL), F32),
        compiler_params=_params("parallel"),
        name="merge",
    )(x2, ya, yb, yc, main, main, main, wb, wo)


def _rope_tables(seq):
    rot = 2 * ROPE_HALF
    inv = 1.0 / (ROPE_THETA ** (jnp.arange(0, rot, 2, dtype=F32) / rot))
    ang = jnp.arange(seq, dtype=F32)[:, None] * inv[None, :]
    cos, sin = jnp.cos(ang), jnp.sin(ang)
    pad = DA_QK_DIM - rot
    ones = jnp.ones((seq, pad), F32)
    zeros = jnp.zeros((seq, pad), F32)
    zh = jnp.zeros((seq, ROPE_HALF), F32)
    cosf = jnp.concatenate([cos, cos, ones], axis=1)
    sinm = jnp.concatenate([-sin, zh, zeros], axis=1)
    sinp = jnp.concatenate([zh, sin, zeros], axis=1)
    tile = lambda t: jnp.tile(t, (1, LANES // DA_QK_DIM))
    return tile(cosf), tile(sinm), tile(sinp)


def _block_diag_ones(width, block, dtype):
    i = jnp.arange(width) // block
    return (i[:, None] == i[None, :]).astype(dtype)


def _chunk_tril(size, chunk):
    i = jnp.arange(size)
    same = (i[:, None] // chunk) == (i[None, :] // chunk)
    return (same & (i[None, :] <= i[:, None])).astype(BF16)


def _reorder_w_in(w):
    cols = lambda lo, n: w[:, lo:lo + n]
    da = cols(0, 2048)
    rw_rkv = cols(2048, 1536)
    lora = cols(3584, 2 * RW_LORA)
    rw_z = cols(3712, 512)
    rest = cols(4224, 1024 + 3072)
    return jnp.concatenate([da, rw_rkv, rw_z, rest], axis=1).astype(BF16), lora.astype(BF16)


def _constants(seq):
    bd64 = _block_diag_ones(WIDTH, DA_QK_DIM, BF16)
    tri = _chunk_tril(RW_STEP_CHUNKS * CHUNK, CHUNK)
    return dict(rope=_rope_tables(seq), bd64=bd64, bd64x2=jnp.concatenate([bd64, bd64], axis=0),
                tri3=jnp.concatenate([tri, tri, tri], axis=1))


def _layer(x2, mem2, l, p, cst, batch, seq, mem_len):
    row = lambda t: t.reshape(1, -1).astype(F32)
    zpad = jnp.zeros((RW_LORA, WIDTH), F32)

    def hi_hi_lo(w):
        hi = w.astype(BF16)
        return jnp.concatenate([hi, hi, (w - hi.astype(F32)).astype(BF16)], axis=0)

    lam_init = 0.8 - 0.6 * math.exp(-0.3 * l)
    w_main, w_lora = _reorder_w_in(p["w_in"][l])
    qg = row(jnp.tile(p["da_q_norm"][l], WIDTH // DA_QK_DIM)) * (DA_QK_DIM ** -0.5 * LOG2_E)
    kg = row(jnp.tile(p["da_k_norm"][l], WIDTH // DA_QK_DIM))
    km, vm = _mem_kv(mem2, row(p["mem_norm_g"][l]), p["w_mem_kv"][l].astype(BF16),
                     row(p["ca_k_norm"][l]), batch, mem_len)
    main, lora, yc = _in_proj(x2, row(p["norm_g"][l]), w_main, w_lora, qg, kg, *cst["rope"],
                              cst["bd64"], km, vm, row(p["ca_q_norm"][l]), seq, mem_len)

    ya = _diff_attn(main, p["da_lambda"][l].astype(F32), row(p["da_subln"][l]), batch, seq,
                    lam_init)

    mu = p["rw_mu"][l]
    wup = hi_hi_lo(jnp.concatenate([p["rw_w_up"][l], zpad], axis=0))
    aup = hi_hi_lo(jnp.concatenate([zpad, p["rw_a_up"][l]], axis=0))
    prm = (row(mu[:512]), row(mu[512:1024]), row(mu[1024:1536]), row(mu[1536:]),
           row(p["rw_w0"][l]), row(p["rw_a0"][l]), row(p["rw_k_k"][l]), row(p["rw_k_a"][l]),
           row(p["rw_r_k"][l]), row(p["rw_ln_g"][l]), row(p["rw_ln_b"][l]), wup, aup,
           cst["bd64x2"], cst["tri3"])
    yb = _rwkv(main, lora, prm, batch, seq)

    x2 = _merge(x2, ya, yb, yc, main, p["w_branch"][l].astype(BF16), p["w_out"][l].astype(BF16))
    return x2, ya, yb, yc


def kernel(x, mem, norm_g, mem_norm_g, w_in, w_mem_kv, da_q_norm, da_k_norm, da_lambda, da_subln,
           rw_mu, rw_w0, rw_w_up, rw_a0, rw_a_up, rw_k_k, rw_k_a, rw_r_k, rw_ln_g, rw_ln_b,
           ca_q_norm, ca_k_norm, w_branch, w_out):
    batch, seq, _ = x.shape
    mem_len = mem.shape[1]
    assert seq % ROW_TILE == 0 and seq % ATT_TILE == 0
    assert seq % (RW_SUBSTEPS * RW_STEP_CHUNKS * CHUNK) == 0 and mem_len % 8 == 0
    p = dict(norm_g=norm_g, mem_norm_g=mem_norm_g, w_in=w_in, w_mem_kv=w_mem_kv,
             da_q_norm=da_q_norm, da_k_norm=da_k_norm, da_lambda=da_lambda, da_subln=da_subln,
             rw_mu=rw_mu, rw_w0=rw_w0, rw_w_up=rw_w_up, rw_a0=rw_a0, rw_a_up=rw_a_up,
             rw_k_k=rw_k_k, rw_k_a=rw_k_a, rw_r_k=rw_r_k, rw_ln_g=rw_ln_g, rw_ln_b=rw_ln_b,
             ca_q_norm=ca_q_norm, ca_k_norm=ca_k_norm, w_branch=w_branch, w_out=w_out)
    cst = _constants(seq)
    x2 = x.reshape(batch * seq, D_MODEL)
    mem2 = mem.reshape(batch * mem_len, D_MODEL)
    for l in range(w_in.shape[0]):
        x2 = _layer(x2, mem2, l, p, cst, batch, seq, mem_len)[0]
    return x2.reshape(batch, seq, D_MODEL)
```

```python
import functools
import math

import jax
import jax.numpy as jnp
from jax import lax
from jax.experimental import pallas as pl
from jax.experimental.pallas import tpu as pltpu

F32 = jnp.float32
BF16 = jnp.bfloat16

D_MODEL = 1024
WIDTH = 512
DA_HEADS = 4
DA_QK_DIM = 64
DA_V_DIM = 128
RW_HEADS = 8
RW_HEAD = 64
RW_LORA = 64
CA_HEADS = 4
CA_HEAD_DIM = 128
ROPE_THETA = 500000.0
ROPE_HALF = DA_QK_DIM // 8
NORM_EPS = 1e-6
LOG2_E = 1.4426950408889634
GN_EPS = 64e-5

LANES = 128
MAIN_WIDTH = 8192
VMEM_LIMIT = 56 * 1024 * 1024

COL_DA_Q, COL_DA_K, COL_DA_V, COL_DA_Z = 0, 1, 2, 3
COL_RW_R, COL_RW_K, COL_RW_V, COL_RW_Z = 4, 5, 6, 7
COL_CA_Q, COL_CA_Z = 8, 9
COL_GATES = 10

ROW_TILE = 512
MERGE_BANDS = 2
ATT_TILE = 256
CHUNK = 64
RW_STEP_CHUNKS = 4
RW_SUBSTEPS = 2
SIDE_PIECES_PER_MAIN = 2


def _params(*sem):
    return pltpu.CompilerParams(dimension_semantics=sem, vmem_limit_bytes=VMEM_LIMIT)


def _resident(shape):
    nd = len(shape)
    return pl.BlockSpec(shape, lambda *_: (0,) * nd, pipeline_mode=pl.Buffered(1))


def _silu(z):
    return z * (1.0 / (1.0 + jnp.exp(-z)))


def _sigmoid(z):
    return 1.0 / (1.0 + jnp.exp(-z))


def _split3(x):
    hi = x.astype(BF16)
    r1 = x - hi.astype(F32)
    mid = r1.astype(BF16)
    lo = (r1 - mid.astype(F32)).astype(BF16)
    return hi, mid, lo


def _dot(a, b):
    return jnp.dot(a, b, preferred_element_type=F32)


def _dot_nt(a, b):
    return lax.dot_general(a, b, (((1,), (1,)), ((), ())), preferred_element_type=F32)


def _dot_tn(a, b):
    return lax.dot_general(a, b, (((0,), (0,)), ((), ())), preferred_element_type=F32)


def _in_proj_kernel(x_ref, g_ref, w_ref, wlo_ref, qg_ref, kg_ref, cos_ref, sinm_ref, sinp_ref,
                    bd_ref, km_ref, vm_ref, cqg_ref, main_ref, lora_ref, yc_ref):
    x = x_ref[...]
    inv = lax.rsqrt(jnp.mean(x * x, axis=-1, keepdims=True) + NORM_EPS)
    h = (x * inv * g_ref[...]).astype(BF16)
    cosf, sinm, sinp = cos_ref[...], sinm_ref[...], sinp_ref[...]

    ca = {}
    heads = tuple(slice(hd * CA_HEAD_DIM, (hd + 1) * CA_HEAD_DIM) for hd in range(CA_HEADS))

    def ca_queries():
        gain = cqg_ref[...] * (CA_HEAD_DIM ** -0.5 * LOG2_E)
        ca["q"] = []
        for sl in heads:
            q = ca["q_raw"][:, sl]
            q = q * lax.rsqrt(jnp.mean(q * q, axis=-1, keepdims=True) + NORM_EPS) * gain
            ca["q"].append(q.astype(BF16))

    def ca_scores():
        ca["s"] = [_dot_nt(q, km_ref[:, sl]) for q, sl in zip(ca["q"], heads)]

    def ca_probs():
        ca["p"] = [jnp.exp2(s - jnp.max(s, axis=-1, keepdims=True)) for s in ca["s"]]
        ca["l"] = [jnp.sum(p, axis=-1, keepdims=True) for p in ca["p"]]

    def ca_values():
        ca["o"] = [_dot(p.astype(BF16), vm_ref[:, sl]) for p, sl in zip(ca["p"], heads)]

    def ca_store():
        for o, l, sl in zip(ca["o"], ca["l"], heads):
            yc_ref[:, sl] = (o / l * _silu(ca["z"][:, sl])).astype(BF16)

    def head_norm_rope(c, acc, gain_ref):
        st = {}

        def squares():
            st["sq"] = (acc * acc).astype(BF16)

        def sums():
            st["ssq"] = _dot(st["sq"], bd_ref[...])

        def finish():
            y = acc * lax.rsqrt(st["ssq"] * (1.0 / DA_QK_DIM) + NORM_EPS) * gain_ref[...]
            for s in range(WIDTH // LANES):
                t = y[:, s * LANES:(s + 1) * LANES]
                t = (t * cosf + pltpu.roll(t, LANES - ROPE_HALF, 1) * sinm
                     + pltpu.roll(t, ROPE_HALF, 1) * sinp)
                lo = c * WIDTH + s * LANES
                main_ref[:, lo:lo + LANES] = t.astype(BF16)

        return [squares, sums, finish]

    queue = []
    order = (COL_CA_Q, COL_CA_Z, COL_DA_Q, COL_DA_K) + tuple(
        c for c in range(MAIN_WIDTH // WIDTH)
        if c not in (COL_CA_Q, COL_CA_Z, COL_DA_Q, COL_DA_K))
    for c in order:
        acc = _dot(h, w_ref[:, c * WIDTH:(c + 1) * WIDTH])
        if c not in (COL_DA_Q, COL_DA_K):
            main_ref[:, c * WIDTH:(c + 1) * WIDTH] = acc.astype(BF16)
        if c == COL_CA_Q:
            ca["q_raw"] = acc
            queue += [ca_queries, ca_scores, ca_probs, ca_values]
        elif c == COL_CA_Z:
            ca["z"] = acc
            queue.append(ca_store)
        elif c == COL_DA_Q:
            queue += head_norm_rope(c, acc, qg_ref)
        elif c == COL_DA_K:
            queue += head_norm_rope(c, acc, kg_ref)
        if queue:
            queue.pop(0)()
    lora_ref[...] = _dot(h, wlo_ref[...])
    for stage in queue:
        stage()


def _in_proj(x2, g, w_main, w_lora, qg, kg, cosf, sinm, sinp, bd, km, vm, cqg, seq, mem_len):
    tokens = x2.shape[0]
    tm = ROW_TILE
    pos_blocks = seq // tm
    row = lambda i: (i, 0)
    pos = lambda i: (i % pos_blocks, 0)
    mem = lambda i: (i // pos_blocks, 0)
    return pl.pallas_call(
        _in_proj_kernel,
        grid=(tokens // tm,),
        in_specs=[
            pl.BlockSpec((tm, D_MODEL), row),
            _resident((1, D_MODEL)),
            _resident((D_MODEL, MAIN_WIDTH)),
            _resident((D_MODEL, LANES)),
            _resident((1, WIDTH)),
            _resident((1, WIDTH)),
            pl.BlockSpec((tm, LANES), pos),
            pl.BlockSpec((tm, LANES), pos),
            pl.BlockSpec((tm, LANES), pos),
            _resident((WIDTH, WIDTH)),
            pl.BlockSpec((mem_len, WIDTH), mem),
            pl.BlockSpec((mem_len, WIDTH), mem),
            _resident((1, CA_HEAD_DIM)),
        ],
        out_specs=[pl.BlockSpec((tm, MAIN_WIDTH), row), pl.BlockSpec((tm, LANES), row),
                   pl.BlockSpec((tm, WIDTH), row)],
        out_shape=[jax.ShapeDtypeStruct((tokens, MAIN_WIDTH), BF16),
                   jax.ShapeDtypeStruct((tokens, LANES), F32),
                   jax.ShapeDtypeStruct((tokens, WIDTH), BF16)],
        compiler_params=_params("parallel"),
        name="in_proj",
    )(x2, g, w_main, w_lora, qg, kg, cosf, sinm, sinp, bd, km, vm, cqg)


def _diff_attn_kernel(lam_ref, q_ref, k_ref, v_ref, z_ref, sub_ref, o_ref,
                      qs_sc, m_sc, l_sc, acc_sc, s_sc, *, lam_init):
    t = ATT_TILE
    i = pl.program_id(1)
    lane = lax.broadcasted_iota(jnp.int32, (t, LANES), 1)
    heads = tuple(slice(h * LANES, (h + 1) * LANES) for h in range(DA_HEADS))
    for h, sl in enumerate(heads):
        q = q_ref[:, sl]
        zero = jnp.zeros_like(q)
        qs_sc[h] = jnp.concatenate([jnp.where(lane < DA_QK_DIM, q, zero),
                                    jnp.where(lane >= DA_QK_DIM, q, zero)], axis=0)
    def max_step(j0, nb, diagonal):
        ss = {}
        for b in range(nb):
            start = pl.multiple_of((j0 + b) * t, t)
            for h, sl in enumerate(heads):
                s = _dot_nt(qs_sc[h], k_ref[pl.ds(start, t), sl])
                if diagonal:
                    r = lax.broadcasted_iota(jnp.int32, (2 * t, t), 0)
                    r = jnp.where(r >= t, r - t, r)
                    c = lax.broadcasted_iota(jnp.int32, (2 * t, t), 1)
                    s = jnp.where(r >= c, s, -jnp.inf)
                ss[b, h] = s
        for h in range(DA_HEADS):
            fold = None
            for b in range(nb):
                s = ss[b, h]
                s_sc[j0 + b, h] = s
                fb = jnp.maximum(s[:, :LANES], s[:, LANES:])
                fold = fb if fold is None else jnp.maximum(fold, fb)
            m_sc[h] = fold if diagonal else jnp.maximum(m_sc[h], fold)

    def sum_step(j0, nb, diagonal):
        start = pl.multiple_of(j0 * t, t)
        ps = []
        for h in range(DA_HEADS):
            m = m_sc[h]
            mm = jnp.concatenate([m, m], axis=1)
            part, pb = None, []
            for b in range(nb):
                p = jnp.exp2(s_sc[j0 + b, h] - mm)
                pp = p[:, :LANES] + p[:, LANES:]
                part = pp if part is None else part + pp
                pb.append(p.astype(BF16))
            l_sc[h] = part if diagonal else l_sc[h] + part
            ps.append(pb[0] if nb == 1 else jnp.concatenate(pb, axis=1))
        for h, sl in enumerate(heads):
            pv = _dot(ps[h], v_ref[pl.ds(start, nb * t), sl])
            acc_sc[h] = pv if diagonal else acc_sc[h] + pv

    def loop(step):
        step(i, 1, True)
        pairs = lax.shift_right_logical(i, 1)

        def body(jj, carry):
            step(2 * jj, 2, False)
            return carry
        lax.fori_loop(0, pairs, body, 0)

        @pl.when(jnp.bitwise_and(i, 1) == 1)
        def _():
            step(i - 1, 1, False)

    loop(max_step)
    for h in range(DA_HEADS):
        m_sc[h] = jnp.broadcast_to(jnp.max(m_sc[h], axis=-1, keepdims=True), (2 * t, LANES))
    loop(sum_step)

    lv = lam_ref[...]
    lam = (jnp.exp(jnp.sum(lv[0:1] * lv[1:2], axis=-1, keepdims=True))
           - jnp.exp(jnp.sum(lv[2:3] * lv[3:4], axis=-1, keepdims=True)) + lam_init)
    for h, sl in enumerate(heads):
        num = acc_sc[h]
        den = jnp.sum(l_sc[h], axis=-1, keepdims=True)
        o = num[:t] / den[:t] - lam * (num[t:] / den[t:])
        o = o * lax.rsqrt(jnp.mean(o * o, axis=-1, keepdims=True) + NORM_EPS) * sub_ref[...]
        o = o * (1.0 - lam_init)
        z = z_ref[:, sl].astype(F32)
        o_ref[:, sl] = (o * _silu(z)).astype(BF16)


def _diff_attn(main, lam_vecs, subln, batch, seq, lam_init):
    t = ATT_TILE
    nq = seq // t
    return pl.pallas_call(
        functools.partial(_diff_attn_kernel, lam_init=lam_init),
        grid=(batch, nq),
        in_specs=[
            _resident((4, DA_QK_DIM)),
            pl.BlockSpec((t, WIDTH), lambda b, i: (b * nq + i, COL_DA_Q)),
            pl.BlockSpec((seq, WIDTH), lambda b, i: (b, COL_DA_K)),
            pl.BlockSpec((seq, WIDTH), lambda b, i: (b, COL_DA_V)),
            pl.BlockSpec((t, WIDTH), lambda b, i: (b * nq + i, COL_DA_Z)),
            _resident((1, DA_V_DIM)),
        ],
        out_specs=pl.BlockSpec((t, WIDTH), lambda b, i: (b * nq + i, 0)),
        out_shape=jax.ShapeDtypeStruct((batch * seq, WIDTH), BF16),
        scratch_shapes=[pltpu.VMEM((DA_HEADS, 2 * t, LANES), BF16),
                        pltpu.VMEM((DA_HEADS, 2 * t, LANES), F32),
                        pltpu.VMEM((DA_HEADS, 2 * t, LANES), F32),
                        pltpu.VMEM((DA_HEADS, 2 * t, DA_V_DIM), F32),
                        pltpu.VMEM((nq, DA_HEADS, 2 * t, t), F32)],
        compiler_params=_params("parallel", "arbitrary"),
        name="diff_attn",
    )(lam_vecs, main, main, main, main, subln)


def _rwkv_kernel(r_ref, k_ref, v_ref, z_ref, lo_ref, mur_ref, muk_ref, muv_ref, mulo_ref,
                 w0_ref, a0_ref, kk_ref, ka_ref, rk_ref, lng_ref, lnb_ref,
                 wup_ref, aup_ref, bd_ref, tri_ref,
                 o_ref, cr_sc, ck_sc, cv_sc, clo_sc, state_sc):
    @pl.when(pl.program_id(1) == 0)
    def _():
        cr_sc[...] = jnp.zeros(cr_sc.shape, F32)
        ck_sc[...] = jnp.zeros(ck_sc.shape, F32)
        cv_sc[...] = jnp.zeros(cv_sc.shape, F32)
        clo_sc[...] = jnp.zeros(clo_sc.shape, F32)
        state_sc[...] = jnp.zeros(state_sc.shape, F32)

    gts = [state_sc[p] for p in range(RW_HEADS // 2)]
    _software_pipeline([
        _rwkv_rows(sub, gts, r_ref, k_ref, v_ref, z_ref, lo_ref, mur_ref, muk_ref, muv_ref,
                   mulo_ref, w0_ref, a0_ref, kk_ref, ka_ref, rk_ref, lng_ref, lnb_ref, wup_ref,
                   aup_ref, bd_ref, tri_ref, o_ref, cr_sc, ck_sc, cv_sc, clo_sc)
        for sub in range(RW_SUBSTEPS)])
    for p in range(RW_HEADS // 2):
        state_sc[p] = gts[p]


def _software_pipeline(gens):
    nxt = [next(g) for g in gens]

    def emit(s):
        nxt[s] = next(gens[s], None)

    def drain(s, phase):
        while 0 <= s < len(gens) and nxt[s] == phase:
            emit(s)

    drain(0, "prep")
    for s in range(len(gens)):
        while nxt[s] == "main":
            emit(s)
            for _ in range(SIDE_PIECES_PER_MAIN):
                if s + 1 < len(gens) and nxt[s + 1] == "prep":
                    emit(s + 1)
                if s >= 1 and nxt[s - 1] == "tail":
                    emit(s - 1)
        drain(s + 1, "prep")
        drain(s - 1, "tail")
    drain(len(gens) - 1, "tail")


def _rwkv_rows(sub, gts, r_ref, k_ref, v_ref, z_ref, lo_ref, mur_ref, muk_ref, muv_ref, mulo_ref,
               w0_ref, a0_ref, kk_ref, ka_ref, rk_ref, lng_ref, lnb_ref,
               wup_ref, aup_ref, bd_ref, tri_ref,
               o_ref, cr_sc, ck_sc, cv_sc, clo_sc):
    n = CHUNK
    rows = RW_STEP_CHUNKS * n
    blk = pl.ds(sub * rows, rows)
    yield "prep"

    def shift(p, carry_ref, mu):
        first = lax.broadcasted_iota(jnp.int32, p.shape, 0) == 0
        prev = jnp.where(first, carry_ref[...], pltpu.roll(p, 1, 0))
        carry_ref[...] = p[rows - 1:rows, :]
        return p + (prev - p) * mu

    r = shift(r_ref[blk, :].astype(F32), cr_sc, mur_ref[...])
    yield "prep"
    k = shift(k_ref[blk, :].astype(F32), ck_sc, muk_ref[...])
    yield "prep"
    v = shift(v_ref[blk, :].astype(F32), cv_sc, muv_ref[...])
    lo = shift(lo_ref[blk, :], clo_sc, mulo_ref[...])
    yield "prep"

    def dot_hi(x, w3_ref):
        xh = x.astype(BF16)
        xl = (x - xh.astype(F32)).astype(BF16)
        return _dot(jnp.concatenate([xh, xl, xh], axis=1), w3_ref[...])

    wlog = w0_ref[...] + dot_hi(jnp.tanh(lo), wup_ref)
    alog = a0_ref[...] + dot_hi(lo, aup_ref)
    yield "prep"
    tneg = -wlog
    softplus = jnp.maximum(tneg, 0.0) + jnp.log(1.0 + jnp.exp(-jnp.abs(tneg)))
    logdecay = -jnp.exp(-softplus - 0.5)
    a = _sigmoid(alog)
    yield "prep"

    bd = bd_ref[...]
    kk = k * kk_ref[...]
    kk2 = kk * kk
    kk2h = kk2.astype(BF16)
    kk2l = (kk2 - kk2h.astype(F32)).astype(BF16)
    yield "prep"
    ssq = _dot(jnp.concatenate([kk2h, kk2l], axis=1), bd)
    kk = kk / jnp.maximum(jnp.sqrt(ssq), 1e-12)
    kmod = k * (1.0 + (a - 1.0) * ka_ref[...])
    yield "prep"

    yield "prep"
    cum = _dot(tri_ref[...], jnp.concatenate(_split3(logdecay), axis=0))
    yield "prep"
    e_in = jnp.exp(cum)
    e_ex = jnp.exp(cum - logdecay)
    e_inv = jnp.exp(-cum)
    yield "prep"
    a_t = -kk * e_ex
    r_t = r * e_in
    yield "prep"
    b_t = kk * a * e_inv
    k_t = kmod * e_inv
    rkb = r * kmod * rk_ref[...]
    z = z_ref[blk, :].astype(F32)
    lng, lnb = lng_ref[...], lnb_ref[...]
    yield "prep"

    lane = lax.broadcasted_iota(jnp.int32, (n, LANES), 1)
    first_head = lane < RW_HEAD

    def stack(xp):
        zero = jnp.zeros_like(xp)
        return jnp.concatenate([jnp.where(first_head, xp, zero),
                                jnp.where(first_head, zero, xp)], axis=0)

    def per_head(col):
        return jnp.where(first_head, col[:n], col[n:])

    ti = lax.broadcasted_iota(jnp.int32, (n, LANES), 0)
    si = jnp.where(first_head, lane, lane - RW_HEAD)
    strict = ti > si
    incl = ti >= si
    eye = (ti == si).astype(F32)
    same_head = ((lax.broadcasted_iota(jnp.int32, (LANES, LANES), 0) < RW_HEAD)
                 == (lax.broadcasted_iota(jnp.int32, (LANES, LANES), 1) < RW_HEAD))

    chains = [(c, p) for c in range(RW_STEP_CHUNKS) for p in range(RW_HEADS // 2)]

    def operands(c, p):
        rs = slice(c * n, (c + 1) * n)
        sl = slice(p * LANES, (p + 1) * LANES)
        pe = e_in[(c + 1) * n - 1:(c + 1) * n, sl]
        b_p, k_p, v_p = b_t[rs, sl], k_t[rs, sl], v[rs, sl]
        v_pb = v_p.astype(BF16)
        return dict(
            ar=jnp.concatenate([a_t[rs, sl], r_t[rs, sl]], axis=0).astype(BF16),
            a_st=stack(a_t[rs, sl].astype(BF16)), r_p=r_t[rs, sl], v_pb=v_pb, v_st=stack(v_pb),
            pe=pe,
            bk_st=jnp.concatenate([stack(b_p.astype(BF16)), stack(k_p.astype(BF16))], axis=0),
            bk_out=jnp.concatenate([b_p * pe, k_p * pe], axis=0).astype(BF16),
            bonus=per_head(jnp.sum(stack(rkb[rs, sl]), axis=-1, keepdims=True)) * v_p)

    ops = []
    for c, p in chains:
        ops.append(operands(c, p))
        yield "prep" if len(ops) < len(chains) else "main"
    for o in ops:
        m = _dot_nt(o["ar"], o["bk_st"])
        o["x"] = jnp.where(strict, m[:n, :LANES], 0.0)
        o["a_ak"] = jnp.where(strict, m[:n, LANES:], 0.0).astype(BF16)
        o["t_rb"] = jnp.where(incl, m[n:, :LANES], 0.0).astype(BF16)
        o["t_rk"] = jnp.where(incl, m[n:, LANES:], 0.0).astype(BF16)
        o["tinv"] = eye + o["x"]
    yield "main"
    levels = int(math.log2(n)) - 1
    for o in ops:
        xb = o["x"].astype(BF16)
        o["x"] = _dot(xb, stack(xb))
    yield "main"
    for level in range(levels):
        for o in ops:
            xb = o["x"].astype(BF16)
            if level + 1 < levels:
                both = _dot(jnp.concatenate([o["tinv"].astype(BF16), xb], axis=0), stack(xb))
                o["tinv"] = o["tinv"] + both[:n]
                o["x"] = both[n:]
            else:
                o["tinv"] = o["tinv"] + _dot(o["tinv"].astype(BF16), stack(xb))
        yield "main"
    for o in ops:
        o["tb"] = o["tinv"].astype(BF16)
        o["w"] = _dot(o["tb"], o["a_st"]).astype(BF16)
        o["av"] = _dot(o["a_ak"], o["v_st"]).astype(BF16)
    yield "main"
    for o in ops:
        o["u0"] = _dot(o["tb"], stack(o["av"])).astype(BF16)
        o["rw"] = (o["r_p"] + _dot(o["t_rb"], stack(o["w"]))).astype(BF16)
        wbk = _dot_tn(o["w"], o["bk_out"][:n])
        o["wbk"] = jnp.where(same_head, wbk, 0.0).astype(BF16)
    yield "main"
    for o in ops:
        o["y0"] = _dot(jnp.concatenate([o["t_rb"], o["t_rk"]], axis=1),
                       jnp.concatenate([stack(o["u0"]), o["v_st"]], axis=0))
        s0t = _dot_tn(jnp.concatenate([o["u0"], o["v_pb"]], axis=0), o["bk_out"])
        o["s0t"] = jnp.where(same_head, s0t, 0.0)

    pairs = range(RW_HEADS // 2)
    for c in range(RW_STEP_CHUNKS):
        yield "tail"
        for p in pairs:
            o = ops[c * len(pairs) + p]
            gtb = gts[p].astype(BF16)
            o["y"] = _dot_nt(o["rw"], gtb) + o["y0"]
            gts[p] = gts[p] * o["pe"] + _dot(gtb, o["wbk"]) + o["s0t"]
    for c in range(RW_STEP_CHUNKS):
        rs = slice(c * n, (c + 1) * n)
        for p in pairs:
            yield "tail"
            sl = slice(p * LANES, (p + 1) * LANES)
            y = ops[c * len(pairs) + p]["y"]
            mean = per_head(jnp.sum(stack(y), axis=-1, keepdims=True)) * (1.0 / RW_HEAD)
            d = y - mean
            var = per_head(jnp.sum(stack(d * d), axis=-1, keepdims=True)) * (1.0 / RW_HEAD)
            yn = d * lax.rsqrt(var + GN_EPS)
            out = yn * lng[:, sl] + lnb[:, sl] + ops[c * len(pairs) + p]["bonus"]
            o_ref[pl.ds(sub * rows + c * n, n), sl] = (out * _silu(z[rs, sl])).astype(BF16)


def _rwkv(main, lora, prm, batch, seq):
    n = RW_SUBSTEPS * RW_STEP_CHUNKS * CHUNK
    nc = seq // n
    rmap = lambda col: (lambda b, c: (b * nc + c, col))
    vec = lambda: _resident((1, WIDTH))
    return pl.pallas_call(
        _rwkv_kernel,
        grid=(batch, nc),
        in_specs=[
            pl.BlockSpec((n, WIDTH), rmap(COL_RW_R)),
            pl.BlockSpec((n, WIDTH), rmap(COL_RW_K)),
            pl.BlockSpec((n, WIDTH), rmap(COL_RW_V)),
            pl.BlockSpec((n, WIDTH), rmap(COL_RW_Z)),
            pl.BlockSpec((n, LANES), rmap(0)),
            vec(), vec(), vec(), _resident((1, LANES)),
            vec(), vec(), vec(), vec(), vec(), vec(), vec(),
            _resident((3 * LANES, WIDTH)), _resident((3 * LANES, WIDTH)),
            _resident((2 * WIDTH, WIDTH)),
            _resident((RW_STEP_CHUNKS * CHUNK, 3 * RW_STEP_CHUNKS * CHUNK)),
        ],
        out_specs=pl.BlockSpec((n, WIDTH), rmap(0)),
        out_shape=jax.ShapeDtypeStruct((batch * seq, WIDTH), BF16),
        scratch_shapes=[pltpu.VMEM((1, WIDTH), F32), pltpu.VMEM((1, WIDTH), F32),
                        pltpu.VMEM((1, WIDTH), F32), pltpu.VMEM((1, LANES), F32),
                        pltpu.VMEM((RW_HEADS // 2, LANES, LANES), F32)],
        compiler_params=_params("parallel", "arbitrary"),
        name="rwkv7",
    )(main, main, main, main, lora, *prm)


def _mem_kv_kernel(mem_ref, g_ref, w_ref, kg_ref, k_ref, v_ref):
    x = mem_ref[...]
    inv = lax.rsqrt(jnp.mean(x * x, axis=-1, keepdims=True) + NORM_EPS)
    h = (x * inv * g_ref[...]).astype(BF16)
    kv = _dot(h, w_ref[...])
    for hd in range(CA_HEADS):
        sl = slice(hd * CA_HEAD_DIM, (hd + 1) * CA_HEAD_DIM)
        kh = kv[:, sl]
        kh = kh * lax.rsqrt(jnp.mean(kh * kh, axis=-1, keepdims=True) + NORM_EPS) * kg_ref[...]
        k_ref[:, sl] = kh.astype(BF16)
    v_ref[...] = kv[:, WIDTH:].astype(BF16)


def _mem_kv(mem2, g, w, kg, batch, mem_len):
    row = lambda b: (b, 0)
    return pl.pallas_call(
        _mem_kv_kernel,
        grid=(batch,),
        in_specs=[pl.BlockSpec((mem_len, D_MODEL), row), _resident((1, D_MODEL)),
                  _resident((D_MODEL, 2 * WIDTH)), _resident((1, CA_HEAD_DIM))],
        out_specs=[pl.BlockSpec((mem_len, WIDTH), row), pl.BlockSpec((mem_len, WIDTH), row)],
        out_shape=[jax.ShapeDtypeStruct((batch * mem_len, WIDTH), BF16)] * 2,
        compiler_params=_params("parallel"),
        name="mem_kv",
    )(mem2, g, w, kg)


def _merge_kernel(x_ref, ya_ref, yb_ref, yc_ref, ga_ref, gb_ref, gc_ref, wb_ref, wo_ref, o_ref):
    def branches(rows):
        merged = None
        for n, (y_ref, g_ref) in enumerate(((ya_ref, ga_ref), (yb_ref, gb_ref), (yc_ref, gc_ref))):
            term = _sigmoid(g_ref[rows, :].astype(F32)) * _dot(y_ref[rows, :], wb_ref[n])
            merged = term if merged is None else merged + term
        return merged.astype(BF16)

    def project(rows, merged):
        o_ref[rows, :] = x_ref[rows, :] + _dot(merged, wo_ref[...])

    bands = [pl.ds(b * ROW_TILE, ROW_TILE) for b in range(MERGE_BANDS)]
    pending = None
    for rows in bands:
        merged = branches(rows)
        if pending is not None:
            project(*pending)
        pending = (rows, merged)
    project(*pending)


def _merge(x2, ya, yb, yc, main, wb, wo):
    tokens = x2.shape[0]
    tm = MERGE_BANDS * ROW_TILE
    row = lambda i: (i, 0)
    gate = lambda n: (lambda i: (i, COL_GATES // 2 + n))
    return pl.pallas_call(
        _merge_kernel,
        grid=(tokens // tm,),
        in_specs=[
            pl.BlockSpec((tm, D_MODEL), row),
            pl.BlockSpec((tm, WIDTH), row), pl.BlockSpec((tm, WIDTH), row),
            pl.BlockSpec((tm, WIDTH), row),
            pl.BlockSpec((tm, D_MODEL), gate(0)), pl.BlockSpec((tm, D_MODEL), gate(1)),
            pl.BlockSpec((tm, D_MODEL), gate(2)),
            _resident((3, WIDTH, D_MODEL)), _resident((D_MODEL, D_MODEL)),
        ],
        out_specs=pl.BlockSpec((tm, D_MODEL), row),
        out_shape=jax.ShapeDtypeStruct((tokens, D_MODEL), F32),
        compiler_params=_params("parallel"),
        name="merge",
    )(x2, ya, yb, yc, main, main, main, wb, wo)


def _rope_tables(seq):
    rot = 2 * ROPE_HALF
    inv = 1.0 / (ROPE_THETA ** (jnp.arange(0, rot, 2, dtype=F32) / rot))
    ang = jnp.arange(seq, dtype=F32)[:, None] * inv[None, :]
    cos, sin = jnp.cos(ang), jnp.sin(ang)
    pad = DA_QK_DIM - rot
    ones = jnp.ones((seq, pad), F32)
    zeros = jnp.zeros((seq, pad), F32)
    zh = jnp.zeros((seq, ROPE_HALF), F32)
    cosf = jnp.concatenate([cos, cos, ones], axis=1)
    sinm = jnp.concatenate([-sin, zh, zeros], axis=1)
    sinp = jnp.concatenate([zh, sin, zeros], axis=1)
    tile = lambda t: jnp.tile(t, (1, LANES // DA_QK_DIM))
    return tile(cosf), tile(sinm), tile(sinp)


def _block_diag_ones(width, block, dtype):
    i = jnp.arange(width) // block
    return (i[:, None] == i[None, :]).astype(dtype)


def _chunk_tril(size, chunk):
    i = jnp.arange(size)
    same = (i[:, None] // chunk) == (i[None, :] // chunk)
    return (same & (i[None, :] <= i[:, None])).astype(BF16)


def _reorder_w_in(w):
    cols = lambda lo, n: w[:, lo:lo + n]
    da = cols(0, 2048)
    rw_rkv = cols(2048, 1536)
    lora = cols(3584, 2 * RW_LORA)
    rw_z = cols(3712, 512)
    rest = cols(4224, 1024 + 3072)
    return jnp.concatenate([da, rw_rkv, rw_z, rest], axis=1).astype(BF16), lora.astype(BF16)


def _constants(seq):
    bd64 = _block_diag_ones(WIDTH, DA_QK_DIM, BF16)
    tri = _chunk_tril(RW_STEP_CHUNKS * CHUNK, CHUNK)
    return dict(rope=_rope_tables(seq), bd64=bd64, bd64x2=jnp.concatenate([bd64, bd64], axis=0),
                tri3=jnp.concatenate([tri, tri, tri], axis=1))


def _layer(x2, mem2, l, p, cst, batch, seq, mem_len):
    row = lambda t: t.reshape(1, -1).astype(F32)
    zpad = jnp.zeros((RW_LORA, WIDTH), F32)

    def hi_hi_lo(w):
        hi = w.astype(BF16)
        return jnp.concatenate([hi, hi, (w - hi.astype(F32)).astype(BF16)], axis=0)

    lam_init = 0.8 - 0.6 * math.exp(-0.3 * l)
    w_main, w_lora = _reorder_w_in(p["w_in"][l])
    qg = row(jnp.tile(p["da_q_norm"][l], WIDTH // DA_QK_DIM)) * (DA_QK_DIM ** -0.5 * LOG2_E)
    kg = row(jnp.tile(p["da_k_norm"][l], WIDTH // DA_QK_DIM))
    km, vm = _mem_kv(mem2, row(p["mem_norm_g"][l]), p["w_mem_kv"][l].astype(BF16),
                     row(p["ca_k_norm"][l]), batch, mem_len)
    main, lora, yc = _in_proj(x2, row(p["norm_g"][l]), w_main, w_lora, qg, kg, *cst["rope"],
                              cst["bd64"], km, vm, row(p["ca_q_norm"][l]), seq, mem_len)

    ya = _diff_attn(main, p["da_lambda"][l].astype(F32), row(p["da_subln"][l]), batch, seq,
                    lam_init)

    mu = p["rw_mu"][l]
    wup = hi_hi_lo(jnp.concatenate([p["rw_w_up"][l], zpad], axis=0))
    aup = hi_hi_lo(jnp.concatenate([zpad, p["rw_a_up"][l]], axis=0))
    prm = (row(mu[:512]), row(mu[512:1024]), row(mu[1024:1536]), row(mu[1536:]),
           row(p["rw_w0"][l]), row(p["rw_a0"][l]), row(p["rw_k_k"][l]), row(p["rw_k_a"][l]),
           row(p["rw_r_k"][l]), row(p["rw_ln_g"][l]), row(p["rw_ln_b"][l]), wup, aup,
           cst["bd64x2"], cst["tri3"])
    yb = _rwkv(main, lora, prm, batch, seq)

    x2 = _merge(x2, ya, yb, yc, main, p["w_branch"][l].astype(BF16), p["w_out"][l].astype(BF16))
    return x2, ya, yb, yc


def kernel(x, mem, norm_g, mem_norm_g, w_in, w_mem_kv, da_q_norm, da_k_norm, da_lambda, da_subln,
           rw_mu, rw_w0, rw_w_up, rw_a0, rw_a_up, rw_k_k, rw_k_a, rw_r_k, rw_ln_g, rw_ln_b,
           ca_q_norm, ca_k_norm, w_branch, w_out):
    batch, seq, _ = x.shape
    mem_len = mem.shape[1]
    assert seq % ROW_TILE == 0 and seq % ATT_TILE == 0
    assert seq % (RW_SUBSTEPS * RW_STEP_CHUNKS * CHUNK) == 0 and mem_len % 8 == 0
    p = dict(norm_g=norm_g, mem_norm_g=mem_norm_g, w_in=w_in, w_mem_kv=w_mem_kv,
             da_q_norm=da_q_norm, da_k_norm=da_k_norm, da_lambda=da_lambda, da_subln=da_subln,
             rw_mu=rw_mu, rw_w0=rw_w0, rw_w_up=rw_w_up, rw_a0=rw_a0, rw_a_up=rw_a_up,
             rw_k_k=rw_k_k, rw_k_a=rw_k_a, rw_r_k=rw_r_k, rw_ln_g=rw_ln_g, rw_ln_b=rw_ln_b,
             ca_q_norm=ca_q_norm, ca_k_norm=ca_k_norm, w_branch=w_branch, w_out=w_out)
    cst = _constants(seq)
    x2 = x.reshape(batch * seq, D_MODEL)
    mem2 = mem.reshape(batch * mem_len, D_MODEL)
    for l in range(w_in.shape[0]):
        x2 = _layer(x2, mem2, l, p, cst, batch, seq, mem_len)[0]
    return x2.reshape(batch, seq, D_MODEL)
```
